```python
import math
import jax, jax.numpy as jnp
from jax import lax
import numpy as np

D_MODEL = 1024
BATCH = 8
SEQ = 4096
DEPTH = 2

GRID_W = 64
CTX_LEN = 256
EPS = 1e-6
ROPE_THETA = 10000.0
F32 = jnp.float32

A_HEADS = 4
A_QK_DIM = 64
A_V_DIM = 2 * A_QK_DIM
A_WIDTH = A_HEADS * A_V_DIM
A_PROJ = 2 * (A_HEADS * 2 * A_QK_DIM) + A_WIDTH
Q_BLOCK = 128

B_HEADS = 8
B_HEAD = 64
B_WIDTH = B_HEADS * B_HEAD
B_DECAY_LORA = 64
B_AAA_LORA = 64
B_GATE_LORA = 128
B_GN_EPS = 64e-5
B_PROJ = 3 * B_WIDTH + B_DECAY_LORA + B_AAA_LORA + B_GATE_LORA

C_HEADS = 8
C_HEAD = 64
C_WIDTH = C_HEADS * C_HEAD
C_GROUPS = 2
C_REP = C_HEADS // C_GROUPS
C_STATE = 128
C_CONV = 5
C_CHUNK = 128
C_CONV_DIM = C_WIDTH + 2 * C_GROUPS * C_STATE
C_PROJ = C_WIDTH + C_CONV_DIM + 2 * C_HEADS

D_HEADS = 4
D_KEY = 64
D_VAL = 128
D_WIDTH = D_HEADS * D_VAL
D_GATE_LORA = 16
D_GATE_TAU = 16.0
D_CHUNK = 64
D_PROJ = 2 * D_HEADS * D_KEY + 2 * D_WIDTH + D_GATE_LORA

MLP_HIDDEN = 4 * D_MODEL
EVEN_IN = A_PROJ + B_PROJ
ODD_IN = C_PROJ + D_PROJ
MIX_WIDTH = A_WIDTH + B_WIDTH

kernel_name = 'hybrid_diffattn_rwkv7_ssd_gla_dit'


def rmsnorm(x, g, eps=EPS):
    xf = x.astype(F32)
    y = xf * lax.rsqrt(jnp.mean(xf * xf, axis=-1, keepdims=True) + eps)
    return (y * g.astype(F32)).astype(x.dtype)


def _split(t, sizes):
    idx = [int(s) for s in np.cumsum(sizes)[:-1]]
    return jnp.split(t, idx, axis=-1)


def _flip_seq(*ts):
    return tuple(jnp.flip(t, axis=1) for t in ts)


def axial_rope_tables(n, dim):
    rows = n // GRID_W
    t = jnp.arange(rows * GRID_W)
    row = (t // GRID_W).astype(F32)
    col = (t % GRID_W).astype(F32)
    nf = dim // 4
    inv = ROPE_THETA ** (-jnp.arange(nf, dtype=F32) / nf)
    ar = row[:, None] * inv[None, :]
    ac = col[:, None] * inv[None, :]
    return jnp.cos(ar), jnp.sin(ar), jnp.cos(ac), jnp.sin(ac)


def _rot_half(x, cos, sin):
    x1, x2 = jnp.split(x, 2, axis=-1)
    return jnp.concatenate([x1 * cos - x2 * sin, x2 * cos + x1 * sin], axis=-1)


def apply_axial_rope(x, tabs):
    cr, sr, cc, sc = tabs
    xr, xc = jnp.split(x.astype(F32), 2, axis=-1)
    return jnp.concatenate([_rot_half(xr, cr, sr), _rot_half(xc, cc, sc)], axis=-1)


def _diff_softmax_mix(q, k, v, lam):
    s = jnp.einsum('bhmqd,bhmkd->bhmqk', q.astype(F32), k.astype(F32)) * (A_QK_DIM ** -0.5)
    p = jax.nn.softmax(s, axis=-1)
    w = p[:, :, 0] - lam * p[:, :, 1]
    return jnp.einsum('bhqk,bhkd->bhqd', w, v.astype(F32))


def diff_attention(t_lat, t_ctx, lam_q1, lam_k1, lam_q2, lam_k2, subln, layer_idx, need_ctx):
    bsz, n = t_lat.shape[0], t_lat.shape[1]
    lam_init = 0.8 - 0.6 * math.exp(-0.3 * layer_idx)
    lam = (jnp.exp(jnp.sum(lam_q1.astype(F32) * lam_k1.astype(F32)))
           - jnp.exp(jnp.sum(lam_q2.astype(F32) * lam_k2.astype(F32))) + lam_init)

    def heads(t):
        m = t.shape[1]
        q, k, v = _split(t, [A_HEADS * 2 * A_QK_DIM, A_HEADS * 2 * A_QK_DIM, A_WIDTH])
        q = q.reshape(bsz, m, A_HEADS, 2, A_QK_DIM).transpose(0, 2, 3, 1, 4)
        k = k.reshape(bsz, m, A_HEADS, 2, A_QK_DIM).transpose(0, 2, 3, 1, 4)
        v = v.reshape(bsz, m, A_HEADS, A_V_DIM).transpose(0, 2, 1, 3)
        return q, k, v

    q_l, k_l, v_l = heads(t_lat)
    q_c, k_c, v_c = heads(t_ctx)
    tabs = axial_rope_tables(n, A_QK_DIM)
    q_l = apply_axial_rope(q_l, tabs)
    k_l = apply_axial_rope(k_l, tabs)
    k_all = jnp.concatenate([k_c.astype(F32), k_l], axis=3)
    v_all = jnp.concatenate([v_c, v_l], axis=2)
    nb = n // Q_BLOCK
    qb = jnp.moveaxis(q_l.reshape(bsz, A_HEADS, 2, nb, Q_BLOCK, A_QK_DIM), 3, 0)
    ob = lax.map(lambda qq: _diff_softmax_mix(qq, k_all, v_all, lam), qb)
    o_l = jnp.moveaxis(ob, 0, 2).reshape(bsz, A_HEADS, n, A_V_DIM)

    def post(o):
        o = rmsnorm(o, subln, eps=1e-5) * (1.0 - lam_init)
        return o.transpose(0, 2, 1, 3).reshape(bsz, o.shape[2], A_WIDTH)

    out_l = post(o_l)
    out_c = post(_diff_softmax_mix(q_c, k_c, v_c, lam)) if need_ctx else None
    return out_l, out_c


def _token_shift_mix(f, mu):
    prev = jnp.pad(f[:, :-1], ((0, 0), (1, 0), (0, 0)))
    nxt = jnp.pad(f[:, 1:], ((0, 0), (0, 1), (0, 0)))
    return f + (0.5 * (prev + nxt) - f) * mu


def _rwkv7_scan(S0, r, w, k, v, kk, a):
    def step(S, inp):
        r_t, w_t, k_t, v_t, kk_t, a_t = inp
        sa = jnp.einsum('bhvk,bhk->bhv', S, kk_t)
        S = (S * w_t[:, :, None, :] - sa[..., None] * (kk_t * a_t)[:, :, None, :]
             + v_t[..., None] * k_t[:, :, None, :])
        return S, jnp.einsum('bhvk,bhk->bhv', S, r_t)
    xs = tuple(jnp.swapaxes(t, 0, 1) for t in (r, w, k, v, kk, a))
    S, ys = lax.scan(step, S0, xs)
    return jnp.swapaxes(ys, 0, 1), S


def rwkv7_mix(t_lat, t_ctx, mu, w0_f, w2_f, w0_b, w2_b, a0, a2, g2, k_k, k_a, r_k, lnx_w, lnx_b, need_ctx):
    def prep(t):
        bsz, n = t.shape[0], t.shape[1]
        f = _token_shift_mix(t, mu).astype(F32)
        r, k, v, wd, ad, gd = _split(f, [B_WIDTH, B_WIDTH, B_WIDTH, B_DECAY_LORA, B_AAA_LORA, B_GATE_LORA])
        hd = lambda u: u.reshape(bsz, n, B_HEADS, B_HEAD)
        a = jax.nn.sigmoid(a0 + ad @ a2)
        tw = jnp.tanh(wd)
        def decay(w0, w2):
            wl = -jax.nn.softplus(-(w0 + tw @ w2)) - 0.5
            return hd(jnp.exp(-jnp.exp(wl)))
        kk = hd(k * k_k)
        kk = kk / jnp.maximum(jnp.sqrt(jnp.sum(kk * kk, axis=-1, keepdims=True)), 1e-12)
        k = k * (1.0 + (a - 1.0) * k_a)
        g = jax.nn.sigmoid(gd) @ g2
        return (hd(r), hd(k), hd(v), kk, hd(a), decay(w0_f, w2_f), decay(w0_b, w2_b), g)

    r_c, k_c, v_c, kk_c, a_c, wf_c, wb_c, g_c = prep(t_ctx)
    r_l, k_l, v_l, kk_l, a_l, wf_l, wb_l, g_l = prep(t_lat)
    S0 = jnp.zeros((t_lat.shape[0], B_HEADS, B_HEAD, B_HEAD), F32)
    yf_c, Sf_c = _rwkv7_scan(S0, r_c, wf_c, k_c, v_c, kk_c, a_c)
    yb_c, Sb_c = _rwkv7_scan(S0, *_flip_seq(r_c, wb_c, k_c, v_c, kk_c, a_c))
    yf_l, _ = _rwkv7_scan(Sf_c, r_l, wf_l, k_l, v_l, kk_l, a_l)
    yb_l, _ = _rwkv7_scan(Sb_c, *_flip_seq(r_l, wb_l, k_l, v_l, kk_l, a_l))

    def post(r, k, v, g, yf, yb):
        bsz, n = r.shape[0], r.shape[1]
        y = yf + jnp.flip(yb, axis=1)
        mean = jnp.mean(y, axis=-1, keepdims=True)
        var = jnp.mean(jnp.square(y - mean), axis=-1, keepdims=True)
        y = ((y - mean) * lax.rsqrt(var + B_GN_EPS)).reshape(bsz, n, B_WIDTH) * lnx_w + lnx_b
        bonus = jnp.sum(r * k * r_k.reshape(B_HEADS, B_HEAD), axis=-1, keepdims=True) * v
        return (y + bonus.reshape(bsz, n, B_WIDTH)) * g

    out_l = post(r_l, k_l, v_l, g_l, yf_l, yb_l)
    out_c = post(r_c, k_c, v_c, g_c, yf_c, yb_c) if need_ctx else None
    return out_l, out_c


def _dwconv_centred(t, w, b):
    y = lax.conv_general_dilated(t, w[:, None, :].astype(t.dtype), window_strides=(1,),
                                 padding=[(C_CONV // 2, C_CONV // 2)],
                                 dimension_numbers=('NWC', 'WIO', 'NWC'),
                                 feature_group_count=t.shape[-1])
    return y + b


def _ssd_chunk_scan(h0, x, dt, la, bm, cm):
    bsz, n = x.shape[0], x.shape[1]
    nc = n // C_CHUNK
    chunks = lambda t: jnp.swapaxes(t.reshape((bsz, nc, C_CHUNK) + t.shape[2:]), 0, 1)
    idx = jnp.arange(C_CHUNK)
    lower = (idx[:, None] >= idx[None, :])[None, :, :, None, None]

    def step(h, inp):
        xc, dtc, lac, bc, cc = inp
        cum = jnp.cumsum(lac, axis=1)
        lmat = jnp.exp(jnp.where(lower, cum[:, :, None] - cum[:, None, :], -jnp.inf))
        cb = jnp.einsum('bign,bjgn->bijg', cc, bc)
        xdt = xc * dtc[..., None]
        y = jnp.einsum('bijgr,bjgrp->bigrp', cb[..., None] * lmat, xdt)
        y = y + jnp.einsum('bign,bgrpn->bigrp', cc, h) * jnp.exp(cum)[..., None]
        w_end = jnp.exp(cum[:, -1:] - cum)
        h = h * jnp.exp(cum[:, -1])[..., None, None] + jnp.einsum('bjgn,bjgrp->bgrpn', bc, xdt * w_end[..., None])
        return h, y

    h, ys = lax.scan(step, h0, tuple(chunks(t) for t in (x, dt, la, bm, cm)))
    return jnp.swapaxes(ys, 0, 1).reshape(x.shape), h


def ssd_mix(t_lat, t_ctx, conv_w, conv_b, dt_bias_f, a_log_f, dt_bias_b, a_log_b, d_skip, norm_g, need_ctx):
    def prep(t):
        bsz, n = t.shape[0], t.shape[1]
        z, xbc, dt_f, dt_b = _split(t, [C_WIDTH, C_CONV_DIM, C_HEADS, C_HEADS])
        xbc = jax.nn.silu(_dwconv_centred(xbc, conv_w, conv_b).astype(F32))
        xs, bm, cm = _split(xbc, [C_WIDTH, C_GROUPS * C_STATE, C_GROUPS * C_STATE])
        xs = xs.reshape(bsz, n, C_GROUPS, C_REP, C_HEAD)
        bm = bm.reshape(bsz, n, C_GROUPS, C_STATE)
        cm = cm.reshape(bsz, n, C_GROUPS, C_STATE)
        def direction(dt_raw, dt_bias, a_log):
            dt = jax.nn.softplus(dt_raw.astype(F32) + dt_bias).reshape(bsz, n, C_GROUPS, C_REP)
            return dt, dt * (-jnp.exp(a_log.astype(F32))).reshape(C_GROUPS, C_REP)
        return z, xs, bm, cm, direction(dt_f, dt_bias_f, a_log_f), direction(dt_b, dt_bias_b, a_log_b)

    z_c, x_c, b_c, c_c, (dtf_c, laf_c), (dtb_c, lab_c) = prep(t_ctx)
    z_l, x_l, b_l, c_l, (dtf_l, laf_l), (dtb_l, lab_l) = prep(t_lat)
    h0 = jnp.zeros((t_lat.shape[0], C_GROUPS, C_REP, C_HEAD, C_STATE), F32)
    yf_c, hf_c = _ssd_chunk_scan(h0, x_c, dtf_c, laf_c, b_c, c_c)
    yb_c, hb_c = _ssd_chunk_scan(h0, *_flip_seq(x_c, dtb_c, lab_c, b_c, c_c))
    yf_l, _ = _ssd_chunk_scan(hf_c, x_l, dtf_l, laf_l, b_l, c_l)
    yb_l, _ = _ssd_chunk_scan(hb_c, *_flip_seq(x_l, dtb_l, lab_l, b_l, c_l))

    def post(z, xs, yf, yb):
        bsz, n = xs.shape[0], xs.shape[1]
        y = yf + jnp.flip(yb, axis=1) + xs * d_skip.astype(F32).reshape(C_GROUPS, C_REP, 1)
        y = y.reshape(bsz, n, C_WIDTH) * jax.nn.silu(z.astype(F32))
        y = rmsnorm(y.reshape(bsz, n, C_GROUPS, C_WIDTH // C_GROUPS), norm_g.reshape(C_GROUPS, C_WIDTH // C_GROUPS))
        return y.reshape(bsz, n, C_WIDTH)

    out_l = post(z_l, x_l, yf_l, yb_l)
    out_c = post(z_c, x_c, yf_c, yb_c) if need_ctx else None
    return out_l, out_c


def _gla_chunk_scan(S0, q, k, v, lg):
    bsz, n = q.shape[0], q.shape[1]
    nc = n // D_CHUNK
    chunks = lambda t: jnp.swapaxes(t.reshape((bsz, nc, D_CHUNK) + t.shape[2:]), 0, 1)
    idx = jnp.arange(D_CHUNK)
    lower = (idx[:, None] >= idx[None, :])[None, :, :, None, None]

    def step(S, inp):
        qc, kc, vc, gc = inp
        cum = jnp.cumsum(gc, axis=1)
        dec = jnp.exp(jnp.where(lower, cum[:, :, None] - cum[:, None, :], -jnp.inf))
        att = jnp.einsum('bihk,bjhk,bijhk->bhij', qc, kc, dec)
        y = jnp.einsum('bhij,bjhv->bihv', att, vc) + jnp.einsum('bihk,bhkv->bihv', qc * jnp.exp(cum), S)
        S = S * jnp.exp(cum[:, -1])[..., None] + jnp.einsum('bjhk,bjhv->bhkv', kc * jnp.exp(cum[:, -1:] - cum), vc)
        return S, y

    S, ys = lax.scan(step, S0, tuple(chunks(t) for t in (q, k, v, lg)))
    return jnp.swapaxes(ys, 0, 1).reshape(v.shape), S


def gla_mix(t_lat, t_ctx, gk_up_f, gk_b_f, gk_up_b, gk_b_b, norm_g, need_ctx):
    def prep(t):
        bsz, n = t.shape[0], t.shape[1]
        q, k, v, g, gd = _split(t.astype(F32), [D_HEADS * D_KEY, D_HEADS * D_KEY, D_WIDTH, D_WIDTH, D_GATE_LORA])
        hk = lambda u: u.reshape(bsz, n, D_HEADS, D_KEY)
        def log_gate(up, b):
            return hk(jax.nn.log_sigmoid(gd @ up + b) / D_GATE_TAU)
        return (hk(q) * (D_KEY ** -0.5), hk(k), v.reshape(bsz, n, D_HEADS, D_VAL), g,
                log_gate(gk_up_f, gk_b_f), log_gate(gk_up_b, gk_b_b))

    q_c, k_c, v_c, g_c, gf_c, gb_c = prep(t_ctx)
    q_l, k_l, v_l, g_l, gf_l, gb_l = prep(t_lat)
    S0 = jnp.zeros((t_lat.shape[0], D_HEADS, D_KEY, D_VAL), F32)
    yf_c, Sf_c = _gla_chunk_scan(S0, q_c, k_c, v_c, gf_c)
    yb_c, Sb_c = _gla_chunk_scan(S0, *_flip_seq(q_c, k_c, v_c, gb_c))
    yf_l, _ = _gla_chunk_scan(Sf_c, q_l, k_l, v_l, gf_l)
    yb_l, _ = _gla_chunk_scan(Sb_c, *_flip_seq(q_l, k_l, v_l, gb_l))

    def post(g, yf, yb):
        bsz, n = g.shape[0], g.shape[1]
        y = rmsnorm(yf + jnp.flip(yb, axis=1), norm_g)
        return y.reshape(bsz, n, D_WIDTH) * jax.nn.silu(g)

    out_l = post(g_l, yf_l, yb_l)
    out_c = post(g_c, yf_c, yb_c) if need_ctx else None
    return out_l, out_c


def even_mixer(p_lat, p_ctx, lam_q1, lam_k1, lam_q2, lam_k2, subln, mu, w0_f, w2_f, w0_b, w2_b,
               a0, a2, g2, k_k, k_a, r_k, lnx_w, lnx_b, layer_idx, need_ctx):
    oa_l, oa_c = diff_attention(p_lat[..., :A_PROJ], p_ctx[..., :A_PROJ],
                                lam_q1, lam_k1, lam_q2, lam_k2, subln, layer_idx, need_ctx)
    ob_l, ob_c = rwkv7_mix(p_lat[..., A_PROJ:], p_ctx[..., A_PROJ:], mu, w0_f, w2_f, w0_b, w2_b,
                           a0, a2, g2, k_k, k_a, r_k, lnx_w, lnx_b, need_ctx)
    o_l = jnp.concatenate([oa_l, ob_l], axis=-1)
    o_c = jnp.concatenate([oa_c, ob_c], axis=-1) if need_ctx else None
    return o_l, o_c


def odd_mixer(p_lat, p_ctx, conv_w, conv_b, dt_bias_f, a_log_f, dt_bias_b, a_log_b, d_skip, ssm_norm,
              gk_up_f, gk_b_f, gk_up_b, gk_b_b, gla_norm, need_ctx):
    oc_l, oc_c = ssd_mix(p_lat[..., :C_PROJ], p_ctx[..., :C_PROJ], conv_w, conv_b, dt_bias_f, a_log_f,
                         dt_bias_b, a_log_b, d_skip, ssm_norm, need_ctx)
    od_l, od_c = gla_mix(p_lat[..., C_PROJ:], p_ctx[..., C_PROJ:], gk_up_f, gk_b_f, gk_up_b, gk_b_b,
                         gla_norm, need_ctx)
    o_l = jnp.concatenate([oc_l, od_l], axis=-1)
    o_c = jnp.concatenate([oc_c, od_c], axis=-1) if need_ctx else None
    return o_l, o_c


def _ada(x, g, shift, scale):
    return rmsnorm(x, g) * (1.0 + scale) + shift


def _sq_relu_mlp(h, w1, w2):
    return jnp.square(jax.nn.relu(h @ w1)) @ w2


def setup_inputs(seed: int = 0) -> dict:
    key = jax.random.key(seed)
    keys = iter(jax.random.split(key, 64))
    D = D_MODEL
    def nrm(shape, scale):
        return jax.random.normal(next(keys), shape, F32) * scale
    def gain(n):
        return 1.0 + 0.05 * jax.random.normal(next(keys), (n,), F32)
    def unif(shape, lo, hi):
        return jax.random.uniform(next(keys), shape, F32, lo, hi)
    def dt_bias(n):
        dt = jnp.exp(unif((n,), math.log(1e-3), math.log(1e-1)))
        return dt + jnp.log(-jnp.expm1(-dt))
    inp = {}
    inp['x'] = nrm((BATCH, SEQ, D), 1.0)
    inp['c'] = nrm((BATCH, D), 1.0)
    inp['ctx'] = nrm((BATCH, CTX_LEN, D), 1.0)
    inp['c_ctx'] = nrm((D,), 1.0)
    inp['l0_mod_w'] = nrm((D, 6 * D), 0.5 * D ** -0.5)
    inp['l0_mod_b'] = nrm((6 * D,), 0.02)
    inp['l0_norm1'] = gain(D)
    inp['l0_norm2'] = gain(D)
    inp['l0_w_in'] = nrm((D, EVEN_IN), D ** -0.5)
    inp['l0_w_out'] = nrm((MIX_WIDTH, D), MIX_WIDTH ** -0.5)
    inp['l0_mlp_w1'] = nrm((D, MLP_HIDDEN), D ** -0.5)
    inp['l0_mlp_w2'] = nrm((MLP_HIDDEN, D), MLP_HIDDEN ** -0.5)
    inp['l0_lam_q1'] = nrm((A_QK_DIM,), 0.1)
    inp['l0_lam_k1'] = nrm((A_QK_DIM,), 0.1)
    inp['l0_lam_q2'] = nrm((A_QK_DIM,), 0.1)
    inp['l0_lam_k2'] = nrm((A_QK_DIM,), 0.1)
    inp['l0_subln'] = gain(A_V_DIM)
    inp['l0_mu'] = unif((B_PROJ,), 0.0, 1.0)
    inp['l0_w0_f'] = unif((B_WIDTH,), -6.0, -1.0)
    inp['l0_w2_f'] = nrm((B_DECAY_LORA, B_WIDTH), 0.1 * B_DECAY_LORA ** -0.5)
    inp['l0_w0_b'] = unif((B_WIDTH,), -6.0, -1.0)
    inp['l0_w2_b'] = nrm((B_DECAY_LORA, B_WIDTH), 0.1 * B_DECAY_LORA ** -0.5)
    inp['l0_a0'] = nrm((B_WIDTH,), 0.1)
    inp['l0_a2'] = nrm((B_AAA_LORA, B_WIDTH), B_AAA_LORA ** -0.5)
    inp['l0_g2'] = nrm((B_GATE_LORA, B_WIDTH), B_GATE_LORA ** -0.5)
    inp['l0_k_k'] = 0.85 + nrm((B_WIDTH,), 0.05)
    inp['l0_k_a'] = gain(B_WIDTH)
    inp['l0_r_k'] = nrm((B_WIDTH,), 0.1)
    inp['l0_lnx_w'] = gain(B_WIDTH)
    inp['l0_lnx_b'] = nrm((B_WIDTH,), 0.02)
    inp['l1_mod_w'] = nrm((D, 6 * D), 0.5 * D ** -0.5)
    inp['l1_mod_b'] = nrm((6 * D,), 0.02)
    inp['l1_norm1'] = gain(D)
    inp['l1_norm2'] = gain(D)
    inp['l1_w_in'] = nrm((D, ODD_IN), D ** -0.5)
    inp['l1_w_out'] = nrm((MIX_WIDTH, D), MIX_WIDTH ** -0.5)
    inp['l1_mlp_w1'] = nrm((D, MLP_HIDDEN), D ** -0.5)
    inp['l1_mlp_w2'] = nrm((MLP_HIDDEN, D), MLP_HIDDEN ** -0.5)
    inp['l1_conv_w'] = nrm((C_CONV, C_CONV_DIM), C_CONV ** -0.5)
    inp['l1_conv_b'] = nrm((C_CONV_DIM,), 0.02)
    inp['l1_dt_bias_f'] = dt_bias(C_HEADS)
    inp['l1_a_log_f'] = jnp.log(unif((C_HEADS,), 1.0, 16.0))
    inp['l1_dt_bias_b'] = dt_bias(C_HEADS)
    inp['l1_a_log_b'] = jnp.log(unif((C_HEADS,), 1.0, 16.0))
    inp['l1_d_skip'] = gain(C_HEADS)
    inp['l1_ssm_norm'] = gain(C_WIDTH)
    inp['l1_gk_up_f'] = nrm((D_GATE_LORA, D_HEADS * D_KEY), D_GATE_LORA ** -0.5)
    inp['l1_gk_b_f'] = nrm((D_HEADS * D_KEY,), 0.1)
    inp['l1_gk_up_b'] = nrm((D_GATE_LORA, D_HEADS * D_KEY), D_GATE_LORA ** -0.5)
    inp['l1_gk_b_b'] = nrm((D_HEADS * D_KEY,), 0.1)
    inp['l1_gla_norm'] = gain(D_VAL)
    inp['final_norm'] = gain(D)
    return inp


def reference(x, c, ctx, c_ctx,
              l0_mod_w, l0_mod_b, l0_norm1, l0_norm2, l0_w_in, l0_w_out, l0_mlp_w1, l0_mlp_w2,
              l0_lam_q1, l0_lam_k1, l0_lam_q2, l0_lam_k2, l0_subln,
              l0_mu, l0_w0_f, l0_w2_f, l0_w0_b, l0_w2_b, l0_a0, l0_a2, l0_g2, l0_k_k, l0_k_a, l0_r_k,
              l0_lnx_w, l0_lnx_b,
              l1_mod_w, l1_mod_b, l1_norm1, l1_norm2, l1_w_in, l1_w_out, l1_mlp_w1, l1_mlp_w2,
              l1_conv_w, l1_conv_b, l1_dt_bias_f, l1_a_log_f, l1_dt_bias_b, l1_a_log_b, l1_d_skip, l1_ssm_norm,
              l1_gk_up_f, l1_gk_b_f, l1_gk_up_b, l1_gk_b_b, l1_gla_norm,
              final_norm):
    shared = [
        (l0_mod_w, l0_mod_b, l0_norm1, l0_norm2, l0_w_in, l0_w_out, l0_mlp_w1, l0_mlp_w2),
        (l1_mod_w, l1_mod_b, l1_norm1, l1_norm2, l1_w_in, l1_w_out, l1_mlp_w1, l1_mlp_w2),
    ]
    for i in range(DEPTH):
        mod_w, mod_b, norm1, norm2, w_in, w_out, mlp_w1, mlp_w2 = shared[i]
        need_ctx = i < DEPTH - 1
        m_l = (jax.nn.silu(c) @ mod_w + mod_b)[:, None, :]
        m_c = jax.nn.silu(c_ctx) @ mod_w + mod_b
        sh1_l, sc1_l, g1_l, sh2_l, sc2_l, g2_l = jnp.split(m_l, 6, axis=-1)
        sh1_c, sc1_c, g1_c, sh2_c, sc2_c, g2_c = jnp.split(m_c, 6, axis=-1)
        p_lat = _ada(x, norm1, sh1_l, sc1_l) @ w_in
        p_ctx = _ada(ctx, norm1, sh1_c, sc1_c) @ w_in
        if i % 2 == 0:
            o_l, o_c = even_mixer(p_lat, p_ctx, l0_lam_q1, l0_lam_k1, l0_lam_q2, l0_lam_k2, l0_subln,
                                  l0_mu, l0_w0_f, l0_w2_f, l0_w0_b, l0_w2_b, l0_a0, l0_a2, l0_g2,
                                  l0_k_k, l0_k_a, l0_r_k, l0_lnx_w, l0_lnx_b, i, need_ctx)
        else:
            o_l, o_c = odd_mixer(p_lat, p_ctx, l1_conv_w, l1_conv_b, l1_dt_bias_f, l1_a_log_f,
                                 l1_dt_bias_b, l1_a_log_b, l1_d_skip, l1_ssm_norm,
                                 l1_gk_up_f, l1_gk_b_f, l1_gk_up_b, l1_gk_b_b, l1_gla_norm, need_ctx)
        x = x + g1_l * (o_l.astype(x.dtype) @ w_out)
        x = x + g2_l * _sq_relu_mlp(_ada(x, norm2, sh2_l, sc2_l), mlp_w1, mlp_w2)
        if need_ctx:
            ctx = ctx + g1_c * (o_c.astype(ctx.dtype) @ w_out)
            ctx = ctx + g2_c * _sq_relu_mlp(_ada(ctx, norm2, sh2_c, sc2_c), mlp_w1, mlp_w2)
    return rmsnorm(x, final_norm)
```

```python
import functools
import math

import jax
import jax.numpy as jnp
from jax import lax
from jax.experimental import pallas as pl
from jax.experimental.pallas import tpu as pltpu

F32 = jnp.float32
BF16 = jnp.bfloat16
HI = lax.Precision.HIGHEST

D = 1024
EPS = 1e-6
GRID_W = 64
ROPE_THETA = 10000.0
MLP_HIDDEN = 4 * D

A_HEADS, A_QK, A_V = 4, 64, 128
A_PROJ = 1536
A_SUBLN_EPS = 1e-5
B_HEADS, B_HEAD = 8, 64
B_WIDTH = 512
B_PROJ = 1792
B_GN_EPS = 64e-5
C_HEADS, C_HEAD, C_GROUPS, C_REP, C_STATE = 8, 64, 2, 4, 128
C_WIDTH = 512
C_CONV = 5
C_CHUNK = 128
C_CONV_DIM = 1024
D_HEADS, D_KEY, D_VAL = 4, 64, 128
D_WIDTH = 512
D_LORA = 16
D_TAU = 16.0
D_CHUNK = 64
ODD_PAD = 3200

TM = 256
TQ = 128
SCAN_TB = 32
HALO = 8
VMEM_LIMIT = 56 * 1024 * 1024


def _cp(*sem):
    return pltpu.CompilerParams(dimension_semantics=sem, vmem_limit_bytes=VMEM_LIMIT)


def _const_spec(shape):
    nd = len(shape)
    return pl.BlockSpec(shape, lambda *_: (0,) * nd, pipeline_mode=pl.Buffered(1))


def _softplus(x):
    return jnp.maximum(x, 0.0) + jnp.log(1.0 + jnp.exp(-jnp.abs(x)))


def _sigmoid(x):
    return 1.0 / (1.0 + jnp.exp(-x))


def _silu(x):
    return x * _sigmoid(x)


def _ada(x, g, sh, sc):
    ms = jnp.mean(x * x, axis=-1, keepdims=True)
    return (x * lax.rsqrt(ms + EPS) * g) * (1.0 + sc) + sh


def _segsum(x, e):
    hi = x.astype(BF16)
    r1 = x - hi.astype(F32)
    mid = r1.astype(BF16)
    lo = (r1 - mid.astype(F32)).astype(BF16)
    d = lambda a: jnp.dot(a, e, preferred_element_type=F32)
    return d(hi) + d(mid) + d(lo)


def _block_ones(n, seg):
    i = jnp.arange(n) // seg
    return (i[:, None] == i[None, :]).astype(BF16)


def _tile_is_lat(nct):
    return lambda b, i: (b, (i >= nct).astype(jnp.int32), 0, 0)


def _mod_kernel(c_ref, w_ref, b_ref, o_ref):
    s = _silu(c_ref[...])
    o_ref[...] = jnp.dot(s, w_ref[...], precision=HI, preferred_element_type=F32) + b_ref[...]


def _modulation(c, c_ctx, mod_w, mod_b):
    bsz = c.shape[0]
    rows = 8 * ((bsz + 1 + 7) // 8)
    cs = jnp.zeros((rows, D), F32).at[0].set(c_ctx).at[1:bsz + 1].set(c)
    n6 = mod_w.shape[1]
    tn = 1024
    m = pl.pallas_call(
        _mod_kernel,
        grid=(n6 // tn,),
        in_specs=[pl.BlockSpec((rows, D), lambda j: (0, 0)),
                  pl.BlockSpec((D, tn), lambda j: (0, j)),
                  pl.BlockSpec((1, tn), lambda j: (0, j))],
        out_specs=pl.BlockSpec((rows, tn), lambda j: (0, j)),
        out_shape=jax.ShapeDtypeStruct((rows, n6), F32),
        compiler_params=_cp("arbitrary"),
    )(cs, mod_w, mod_b.reshape(1, n6))
    m_c = jnp.broadcast_to(m[0:1], (bsz, n6))
    return jnp.stack([m_c, m[1:bsz + 1]], axis=1)[:, :, None, :]


def _rope_tables(n_ctx, n_lat):
    t = jnp.arange(n_lat)
    row = (t // GRID_W).astype(F32)
    col = (t % GRID_W).astype(F32)
    nf = A_QK // 4
    inv = ROPE_THETA ** (-jnp.arange(nf, dtype=F32) / nf)
    ar = row[:, None] * inv[None, :]
    ac = col[:, None] * inv[None, :]
    z = jnp.zeros_like(ar)
    c64 = jnp.concatenate([jnp.cos(ar), jnp.cos(ar), jnp.cos(ac), jnp.cos(ac)], axis=-1)
    s1 = jnp.concatenate([-jnp.sin(ar), z, -jnp.sin(ac), z], axis=-1)
    s2 = jnp.concatenate([z, jnp.sin(ar), z, jnp.sin(ac)], axis=-1)
    pad = lambda a, v: jnp.concatenate([jnp.full((n_ctx, 128), v, F32), jnp.tile(a, (1, 2))], axis=0)
    return pad(c64, 1.0), pad(s1, 0.0), pad(s2, 0.0)


def _inproj0_kernel(x_ref, mod_ref, g_ref, wa_ref, wb_ref, c_ref, s1_ref, s2_ref, qkv_ref, pb_ref):
    mod = mod_ref[...]
    h = _ada(x_ref[...], g_ref[...], mod[:, 0:D], mod[:, D:2 * D]).astype(BF16)
    pa = jnp.dot(h, wa_ref[...], preferred_element_type=F32)
    c, s1, s2 = c_ref[...], s1_ref[...], s2_ref[...]
    scale = A_QK ** -0.5
    for j in range(8):
        blk = pa[:, j * 128:(j + 1) * 128]
        rot = blk * c + pltpu.roll(blk, 112, 1) * s1 + pltpu.roll(blk, 16, 1) * s2
        if j < 4:
            rot = rot * scale
        qkv_ref[:, j * 128:(j + 1) * 128] = rot.astype(BF16)
    qkv_ref[:, 1024:1536] = pa[:, 1024:1536].astype(BF16)
    pb_ref[...] = jnp.dot(h, wb_ref[...], preferred_element_type=F32)


def _inproj0(xc, mod, norm1, w_in, tabs, nct):
    bsz, n, _ = xc.shape
    wa = w_in[:, :A_PROJ].astype(BF16)
    wb = w_in[:, A_PROJ:].astype(BF16)
    tok = lambda w: pl.BlockSpec((None, TM, w), lambda b, i: (b, i, 0))
    tab = pl.BlockSpec((TM, 128), lambda b, i: (i, 0))
    return pl.pallas_call(
        _inproj0_kernel,
        grid=(bsz, n // TM),
        in_specs=[tok(D), pl.BlockSpec((None, None, 1, 6 * D), _tile_is_lat(nct)),
                  _const_spec((1, D)), _const_spec((D, A_PROJ)), _const_spec((D, B_PROJ)),
                  tab, tab, tab],
        out_specs=[tok(A_PROJ), tok(B_PROJ)],
        out_shape=[jax.ShapeDtypeStruct((bsz, n, A_PROJ), BF16),
                   jax.ShapeDtypeStruct((bsz, n, B_PROJ), F32)],
        compiler_params=_cp("parallel", "parallel"),
    )(xc, mod, norm1.reshape(1, D), wa, wb, *tabs)


def _attn_kernel(lam_ref, q_ref, k_ref, v_ref, subln_ref, o_ref, *, lam_init, n_ctx, nq_ctx):
    qi = pl.program_id(2)
    lv = lam_ref[...]
    lam = (jnp.exp(jnp.sum(lv[0:1] * lv[1:2], axis=-1, keepdims=True))
           - jnp.exp(jnp.sum(lv[2:3] * lv[3:4], axis=-1, keepdims=True)) + lam_init)
    q = q_ref[...]
    tq = q.shape[0]
    lane = lax.broadcasted_iota(jnp.int32, q.shape, 1)
    zero = jnp.zeros_like(q)
    q2 = jnp.concatenate([jnp.where(lane < A_QK, q, zero), jnp.where(lane >= A_QK, q, zero)], axis=0)

    def attend(nk):
        k = k_ref[0:nk, :]
        v = v_ref[0:nk, :]
        s = lax.dot_general(q2, k, (((1,), (1,)), ((), ())), preferred_element_type=F32)
        p = jnp.exp(s - jnp.max(s, axis=-1, keepdims=True))
        pn = p / jnp.sum(p, axis=-1, keepdims=True)
        w = pn[:tq] - lam * pn[tq:]
        o = jnp.dot(w.astype(BF16), v, preferred_element_type=F32)
        ms = jnp.mean(o * o, axis=-1, keepdims=True)
        o_ref[...] = (o * lax.rsqrt(ms + A_SUBLN_EPS) * subln_ref[...]) * (1.0 - lam_init)

    @pl.when(qi < nq_ctx)
    def _():
        attend(n_ctx)

    @pl.when(qi >= nq_ctx)
    def _():
        attend(k_ref.shape[0])


def _attention(qkv, lamv, subln, layer_idx, n_ctx):
    bsz, n, _ = qkv.shape
    lam_init = 0.8 - 0.6 * math.exp(-0.3 * layer_idx)
    kern = functools.partial(_attn_kernel, lam_init=lam_init, n_ctx=n_ctx, nq_ctx=n_ctx // TQ)
    return pl.pallas_call(
        kern,
        grid=(bsz, A_HEADS, n // TQ),
        in_specs=[_const_spec((4, A_QK)),
                  pl.BlockSpec((None, TQ, 128), lambda b, h, i: (b, i, h)),
                  pl.BlockSpec((None, n, 128), lambda b, h, i: (b, 0, A_HEADS + h)),
                  pl.BlockSpec((None, n, 128), lambda b, h, i: (b, 0, 2 * A_HEADS + h)),
                  _const_spec((1, A_V))],
        out_specs=pl.BlockSpec((None, TQ, 128), lambda b, h, i: (b, i, h)),
        out_shape=jax.ShapeDtypeStruct((bsz, n, A_HEADS * A_V), F32),
        compiler_params=_cp("parallel", "parallel", "arbitrary"),
    )(lamv, qkv, qkv, qkv, subln.reshape(1, A_V))


def _halo_flags(i, nct, nt):
    prev_ok = jnp.logical_and(i != 0, i != nct)
    next_ok = jnp.logical_and(i != nct - 1, i != nt - 1)
    return prev_ok, next_ok


def _rwkv_prep_kernel(p_ref, pp_ref, pn_ref, mu_ref, vec_ref, w2f_ref, w2b_ref, a2_ref, g2_ref, e_ref,
                      r_ref, wf_ref, wb_ref, k_ref, v_ref, kk_ref, kb_ref, bonus_ref, g_ref, *, nct, nt):
    i = pl.program_id(1)
    f = p_ref[...]
    tm = f.shape[0]
    prev_ok, next_ok = _halo_flags(i, nct, nt)
    hp = jnp.where(prev_ok, pp_ref[HALO - 1:HALO, :], 0.0)
    hn = jnp.where(next_ok, pn_ref[0:1, :], 0.0)
    row = lax.broadcasted_iota(jnp.int32, f.shape, 0)
    prev = jnp.where(row == 0, hp, pltpu.roll(f, 1, 0))
    nxt = jnp.where(row == tm - 1, hn, pltpu.roll(f, tm - 1, 0))
    fm = f + (0.5 * (prev + nxt) - f) * mu_ref[...]
    r = fm[:, 0:512]
    k = fm[:, 512:1024]
    v = fm[:, 1024:1536]
    wd = fm[:, 1536:1600]
    ad = fm[:, 1600:1664]
    gd = fm[:, 1664:1792]
    w0f, w0b, a0 = vec_ref[0:1, :], vec_ref[1:2, :], vec_ref[2:3, :]
    k_k, k_a, r_k = vec_ref[3:4, :], vec_ref[4:5, :], vec_ref[5:6, :]
    e = e_ref[...]
    mm = lambda a, w_ref: jnp.dot(a.astype(BF16), w_ref[...], preferred_element_type=F32)
    a = _sigmoid(a0 + mm(ad, a2_ref))
    tw = jnp.tanh(wd)
    decay = lambda w0, w2_ref: jnp.exp(-jnp.exp(-_softplus(-(w0 + mm(tw, w2_ref))) - 0.5))
    kkr = k * k_k
    kk = kkr / jnp.maximum(jnp.sqrt(_segsum(kkr * kkr, e)), 1e-12)
    k2 = k * (1.0 + (a - 1.0) * k_a)
    r_ref[...] = r
    wf_ref[...] = decay(w0f, w2f_ref)
    wb_ref[...] = decay(w0b, w2b_ref)
    k_ref[...] = k2
    v_ref[...] = v
    kk_ref[...] = kk
    kb_ref[...] = kk * a
    bonus_ref[...] = _segsum(r * k2 * r_k, e) * v
    g_ref[...] = mm(_sigmoid(gd), g2_ref)


def _rwkv_prep(pb, mu, vecs, w2_f, w2_b, a2, g2, nct):
    bsz, n, _ = pb.shape
    nt = n // TM
    hb = TM // HALO
    tok = lambda w: pl.BlockSpec((None, TM, w), lambda b, i: (b, i, 0))
    kern = functools.partial(_rwkv_prep_kernel, nct=nct, nt=nt)
    outs = pl.pallas_call(
        kern,
        grid=(bsz, nt),
        in_specs=[tok(B_PROJ),
                  pl.BlockSpec((None, HALO, B_PROJ), lambda b, i: (b, jnp.maximum(i * hb - 1, 0), 0)),
                  pl.BlockSpec((None, HALO, B_PROJ), lambda b, i: (b, jnp.minimum((i + 1) * hb, nt * hb - 1), 0)),
                  _const_spec((1, B_PROJ)), _const_spec((8, B_WIDTH)),
                  _const_spec((64, B_WIDTH)), _const_spec((64, B_WIDTH)), _const_spec((64, B_WIDTH)),
                  _const_spec((128, B_WIDTH)), _const_spec((B_WIDTH, B_WIDTH))],
        out_specs=[tok(B_WIDTH)] * 9,
        out_shape=[jax.ShapeDtypeStruct((bsz, n, B_WIDTH), F32)] * 9,
        compiler_params=_cp("parallel", "parallel"),
    )(pb, pb, pb, mu.reshape(1, B_PROJ), vecs, w2_f.astype(BF16), w2_b.astype(BF16), a2.astype(BF16),
      g2.astype(BF16), _block_ones(B_WIDTH, B_HEAD))
    return outs


def _rwkv_scan_kernel(r_ref, w_ref, k_ref, v_ref, kk_ref, kb_ref, y_ref, h_scr):
    @pl.when(pl.program_id(0) == 0)
    def _():
        h_scr[...] = jnp.zeros_like(h_scr)

    nk = h_scr.shape[0]
    nacc = 4

    def step(t, carry):
        v = v_ref[t]
        us = [None] * nacc
        for kx in range(nk):
            term = kk_ref[t, pl.ds(kx, 1), :] * h_scr[kx]
            us[kx % nacc] = term if us[kx % nacc] is None else us[kx % nacc] + term
        u = (us[0] + us[1]) + (us[2] + us[3])
        ys = [None] * nacc
        for kx in range(nk):
            hn = (h_scr[kx] * w_ref[t, pl.ds(kx, 1), :]
                  + (k_ref[t, pl.ds(kx, 1), :] * v - kb_ref[t, pl.ds(kx, 1), :] * u))
            h_scr[kx] = hn
            term = r_ref[t, pl.ds(kx, 1), :] * hn
            ys[kx % nacc] = term if ys[kx % nacc] is None else ys[kx % nacc] + term
        y_ref[t] = (ys[0] + ys[1]) + (ys[2] + ys[3])
        return carry

    lax.fori_loop(0, r_ref.shape[0], step, 0)


def _rwkv_scan(r, w, k, v, kk, kb):
    n, hd, lanes = r.shape
    spec = pl.BlockSpec((SCAN_TB, hd, lanes), lambda i: (i, 0, 0))
    return pl.pallas_call(
        _rwkv_scan_kernel,
        grid=(n // SCAN_TB,),
        in_specs=[spec] * 6,
        out_specs=spec,
        out_shape=jax.ShapeDtypeStruct((n, hd, lanes), F32),
        scratch_shapes=[pltpu.VMEM((hd, hd, lanes), F32)],
        compiler_params=_cp("arbitrary"),
    )(r, w, k, v, kk, kb)


def _rwkv_post_kernel(yf_ref, yb_ref, bonus_ref, g_ref, lnw_ref, lnb_ref, e_ref, o_ref):
    e = e_ref[...]
    y = yf_ref[...] + yb_ref[...]
    d = y - _segsum(y, e) * (1.0 / B_HEAD)
    var = _segsum(d * d, e) * (1.0 / B_HEAD)
    yn = d * lax.rsqrt(var + B_GN_EPS) * lnw_ref[...] + lnb_ref[...]
    o_ref[...] = (yn + bonus_ref[...]) * g_ref[...]


def _rwkv_post(yf, yb, bonus, g, lnx_w, lnx_b):
    bsz, n, _ = yf.shape
    tok = pl.BlockSpec((None, TM, B_WIDTH), lambda b, i: (b, i, 0))
    return pl.pallas_call(
        _rwkv_post_kernel,
        grid=(bsz, n // TM),
        in_specs=[tok] * 4 + [_const_spec((1, B_WIDTH)), _const_spec((1, B_WIDTH)),
                              _const_spec((B_WIDTH, B_WIDTH))],
        out_specs=tok,
        out_shape=jax.ShapeDtypeStruct((bsz, n, B_WIDTH), F32),
        compiler_params=_cp("parallel", "parallel"),
    )(yf, yb, bonus, g, lnx_w.reshape(1, -1), lnx_b.reshape(1, -1), _block_ones(B_WIDTH, B_HEAD))


def _flip_parts(t, n_ctx):
    return jnp.concatenate([t[:n_ctx][::-1], t[n_ctx:][::-1]], axis=0)


def _rwkv_mix(pb, n_ctx, mu, w0_f, w2_f, w0_b, w2_b, a0, a2, g2, k_k, k_a, r_k, lnx_w, lnx_b):
    bsz, n, _ = pb.shape
    zero = jnp.zeros_like(w0_f)
    vecs = jnp.stack([w0_f, w0_b, a0, k_k, k_a, r_k, zero, zero], axis=0)
    r, wf, wb, k2, v, kk, kb, bonus, g = _rwkv_prep(pb, mu, vecs, w2_f, w2_b, a2, g2, n_ctx // TM)

    def lanes(a):
        return a.reshape(bsz, n, B_HEADS, B_HEAD).transpose(1, 3, 0, 2).reshape(n, B_HEAD, bsz * B_HEADS)

    def both(a, a_bwd=None):
        t = lanes(a)
        tb = t if a_bwd is None else lanes(a_bwd)
        return jnp.concatenate([t, _flip_parts(tb, n_ctx)], axis=-1)

    y = _rwkv_scan(both(r), both(wf, wb), both(k2), both(v), both(kk), both(kb))
    half = bsz * B_HEADS

    def unlanes(t):
        return t.reshape(n, B_HEAD, bsz, B_HEADS).transpose(2, 0, 3, 1).reshape(bsz, n, B_WIDTH)

    yf = unlanes(y[..., :half])
    yb = unlanes(_flip_parts(y[..., half:], n_ctx))
    return _rwkv_post(yf, yb, bonus, g, lnx_w, lnx_b)


def _post_kernel(x_ref, o1_ref, o2_ref, mod_ref, woa_ref, wob_ref, n2_ref, w1_ref, w2_ref, fn_ref, out_ref,
                 *, final):
    mod = mod_ref[...]
    g1, sh2 = mod[:, 2 * D:3 * D], mod[:, 3 * D:4 * D]
    sc2, g2 = mod[:, 4 * D:5 * D], mod[:, 5 * D:6 * D]
    mix = (jnp.dot(o1_ref[...].astype(BF16), woa_ref[...], preferred_element_type=F32)
           + jnp.dot(o2_ref[...].astype(BF16), wob_ref[...], preferred_element_type=F32))
    x1 = x_ref[...] + g1 * mix
    h = _ada(x1, n2_ref[...], sh2, sc2).astype(BF16)
    hid = jnp.maximum(jnp.dot(h, w1_ref[...], preferred_element_type=F32), 0.0)
    hid = (hid * hid).astype(BF16)
    x2 = x1 + g2 * jnp.dot(hid, w2_ref[...], preferred_element_type=F32)
    if final:
        ms = jnp.mean(x2 * x2, axis=-1, keepdims=True)
        x2 = x2 * lax.rsqrt(ms + EPS) * fn_ref[...]
    out_ref[...] = x2


def _post(xc, o1, o2, mod, w_out, norm2, mlp_w1, mlp_w2, final_norm, nct, lat_only):
    bsz, n, _ = xc.shape
    half = o1.shape[-1]
    t0 = nct if lat_only else 0
    nt = n // TM - t0
    tok = lambda w: pl.BlockSpec((None, TM, w), lambda b, i: (b, i + t0, 0))
    final = final_norm is not None
    fn = (final_norm if final else jnp.ones((D,), F32)).reshape(1, D)
    kern = functools.partial(_post_kernel, final=final)
    return pl.pallas_call(
        kern,
        grid=(bsz, nt),
        in_specs=[tok(D), tok(half), tok(half),
                  pl.BlockSpec((None, None, 1, 6 * D), lambda b, i: (b, (i + t0 >= nct).astype(jnp.int32), 0, 0)),
                  _const_spec((half, D)), _const_spec((half, D)), _const_spec((1, D)),
                  _const_spec((D, MLP_HIDDEN)), _const_spec((MLP_HIDDEN, D)), _const_spec((1, D))],
        out_specs=pl.BlockSpec((None, TM, D), lambda b, i: (b, i, 0)),
        out_shape=jax.ShapeDtypeStruct((bsz, nt * TM, D), F32),
        compiler_params=_cp("parallel", "parallel"),
    )(xc, o1, o2, mod, w_out[:half].astype(BF16), w_out[half:].astype(BF16), norm2.reshape(1, D),
      mlp_w1.astype(BF16), mlp_w2.astype(BF16), fn)


def _inproj1_kernel(x_ref, mod_ref, g_ref, w_ref, z_ref, xbc_ref, qk_ref, v_ref, gg_ref, sm_ref):
    mod = mod_ref[...]
    h = _ada(x_ref[...], g_ref[...], mod[:, 0:D], mod[:, D:2 * D]).astype(BF16)
    p = jnp.dot(h, w_ref[...], preferred_element_type=F32)
    z_ref[...] = p[:, 0:512]
    xbc_ref[...] = p[:, 512:1536]
    qk_ref[...] = p[:, 1536:2048]
    v_ref[...] = p[:, 2048:2560]
    gg_ref[...] = p[:, 2560:3072]
    sm_ref[...] = p[:, 3072:3200]


def _inproj1(xc, mod, norm1, w_in, nct):
    bsz, n, _ = xc.shape
    c0 = C_WIDTH + C_CONV_DIM
    d0 = c0 + 2 * C_HEADS
    qk_w = 2 * D_HEADS * D_KEY
    w = jnp.concatenate([w_in[:, :c0], w_in[:, d0:d0 + qk_w + 2 * D_WIDTH], w_in[:, c0:d0],
                         w_in[:, d0 + qk_w + 2 * D_WIDTH:],
                         jnp.zeros((D, 128 - 2 * C_HEADS - D_LORA), F32)], axis=1).astype(BF16)
    tok = lambda wd: pl.BlockSpec((None, TM, wd), lambda b, i: (b, i, 0))
    widths = [512, 1024, 512, 512, 512, 128]
    return pl.pallas_call(
        _inproj1_kernel,
        grid=(bsz, n // TM),
        in_specs=[tok(D), pl.BlockSpec((None, None, 1, 6 * D), _tile_is_lat(nct)),
                  _const_spec((1, D)), _const_spec((D, ODD_PAD))],
        out_specs=[tok(wd) for wd in widths],
        out_shape=[jax.ShapeDtypeStruct((bsz, n, wd), F32) for wd in widths],
        compiler_params=_cp("parallel", "parallel"),
    )(xc, mod, norm1.reshape(1, D), w)


def _conv_kernel(x_ref, xp_ref, xn_ref, w_ref, b_ref, o_ref, *, nct, nt):
    i = pl.program_id(1)
    x = x_ref[...]
    tm = x.shape[0]
    prev_ok, next_ok = _halo_flags(i, nct, nt)
    hp = jnp.where(prev_ok, xp_ref[...], 0.0)
    hn = jnp.where(next_ok, xn_ref[...], 0.0)
    ext = jnp.concatenate([hp, x, hn], axis=0)
    ne = tm + 2 * HALO
    acc = b_ref[...] + w_ref[2:3, :] * x
    for j in (0, 1, 3, 4):
        shifted = pltpu.roll(ext, (C_CONV // 2 - j) % ne, 0)[HALO:HALO + tm]
        acc = acc + w_ref[j:j + 1, :] * shifted
    o_ref[...] = _silu(acc)


def _conv(xbc, conv_w, conv_b, nct):
    bsz, n, cdim = xbc.shape
    nt = n // TM
    hb = TM // HALO
    tok = pl.BlockSpec((None, TM, cdim), lambda b, i: (b, i, 0))
    w8 = jnp.zeros((8, cdim), F32).at[:C_CONV].set(conv_w)
    return pl.pallas_call(
        functools.partial(_conv_kernel, nct=nct, nt=nt),
        grid=(bsz, nt),
        in_specs=[tok,
                  pl.BlockSpec((None, HALO, cdim), lambda b, i: (b, jnp.maximum(i * hb - 1, 0), 0)),
                  pl.BlockSpec((None, HALO, cdim), lambda b, i: (b, jnp.minimum((i + 1) * hb, nt * hb - 1), 0)),
                  _const_spec((8, cdim)), _const_spec((1, cdim))],
        out_specs=tok,
        out_shape=jax.ShapeDtypeStruct((bsz, n, cdim), F32),
        compiler_params=_cp("parallel", "parallel"),
    )(xbc, xbc, xbc, w8, conv_b.reshape(1, cdim))


def _scan_masks(length, causal):
    ii = lax.broadcasted_iota(jnp.int32, (length, length), 0)
    jj = lax.broadcasted_iota(jnp.int32, (length, length), 1)
    tri = (ii >= jj) if causal else (ii <= jj)
    tri_t = (ii <= jj) if causal else (ii >= jj)
    return tri, tri_t


def _nt_dot(a, b):
    return lax.dot_general(a, b, (((1,), (1,)), ((), ())), preferred_element_type=F32)


def _ssd_dir(xa, sm, sm_t, col0, prm, prm_t, h_scr, y_ref, causal):
    length = xa.shape[0]
    dtb, alog = prm[0:1, col0:col0 + C_HEADS], prm[1:2, col0:col0 + C_HEADS]
    dtb_t, alog_t = prm_t[col0:col0 + C_HEADS, 0:1], prm_t[col0:col0 + C_HEADS, 1:2]
    dt = _softplus(sm[:, col0:col0 + C_HEADS] + dtb)
    la = dt * (-jnp.exp(alog))
    la_t = _softplus(sm_t[col0:col0 + C_HEADS, :] + dtb_t) * (-jnp.exp(alog_t))
    tri, tri_t = _scan_masks(length, causal)
    cum = jnp.dot(tri.astype(F32), la, precision=HI, preferred_element_type=F32)
    cum_t = jnp.dot(la_t, tri_t.astype(F32), precision=HI, preferred_element_type=F32)
    last = length - 1 if causal else 0
    ys = []
    for g in range(C_GROUPS):
        bg = xa[:, C_WIDTH + g * C_STATE:C_WIDTH + (g + 1) * C_STATE].astype(BF16)
        cg = xa[:, C_WIDTH + (C_GROUPS + g) * C_STATE:C_WIDTH + (C_GROUPS + g + 1) * C_STATE].astype(BF16)
        cb = _nt_dot(cg, bg)
        for rr in range(C_REP):
            h = g * C_REP + rr
            ch = cum[:, h:h + 1]
            lmat = jnp.exp(jnp.where(tri, ch - cum_t[h:h + 1, :], -jnp.inf))
            xdt = xa[:, h * C_HEAD:(h + 1) * C_HEAD] * dt[:, h:h + 1]
            hs = h_scr[h]
            y = jnp.dot((cb * lmat).astype(BF16), xdt.astype(BF16), preferred_element_type=F32)
            y = y + _nt_dot(cg, hs.astype(BF16)) * jnp.exp(ch)
            cl = cum[last:last + 1, h:h + 1]
            xw = (xdt * jnp.exp(cl - ch)).T.astype(BF16)
            h_scr[h] = hs * jnp.exp(cl) + jnp.dot(xw, bg, preferred_element_type=F32)
            ys.append(y)
    y_ref[...] = jnp.concatenate(ys, axis=-1)


def _ssd_scan_kernel(xf_ref, smf_ref, smtf_ref, xb_ref, smb_ref, smtb_ref, prm_ref, prmt_ref,
                     yf_ref, yb_ref, hf_scr, hb_scr):
    @pl.when(pl.program_id(1) == 0)
    def _():
        hf_scr[...] = jnp.zeros_like(hf_scr)
        hb_scr[...] = jnp.zeros_like(hb_scr)

    prm, prm_t = prm_ref[...], prmt_ref[...]
    _ssd_dir(xf_ref[...], smf_ref[...], smtf_ref[...], 0, prm, prm_t, hf_scr, yf_ref, True)
    _ssd_dir(xb_ref[...], smb_ref[...], smtb_ref[...], C_HEADS, prm, prm_t, hb_scr, yb_ref, False)


def _bwd_chunk(ncc, nc):
    return lambda s: jnp.where(s < ncc, ncc - 1 - s, nc - 1 - s + ncc)


def _ssd_scan(xact, small, small_t, dt_bias_f, a_log_f, dt_bias_b, a_log_b, n_ctx):
    bsz, n, cdim = xact.shape
    L = C_CHUNK
    nc, ncc = n // L, n_ctx // L
    bw = _bwd_chunk(ncc, nc)
    prm = jnp.zeros((8, 128), F32).at[0, :16].set(jnp.concatenate([dt_bias_f, dt_bias_b]))
    prm = prm.at[1, :16].set(jnp.concatenate([a_log_f, a_log_b]))
    fx = lambda w: pl.BlockSpec((None, L, w), lambda b, s: (b, s, 0))
    bx = lambda w: pl.BlockSpec((None, L, w), lambda b, s: (b, bw(s), 0))
    ft = pl.BlockSpec((None, 128, L), lambda b, s: (b, 0, s))
    bt = pl.BlockSpec((None, 128, L), lambda b, s: (b, 0, bw(s)))
    return pl.pallas_call(
        _ssd_scan_kernel,
        grid=(bsz, nc),
        in_specs=[fx(cdim), fx(128), ft, bx(cdim), bx(128), bt, _const_spec((8, 128)), _const_spec((128, 8))],
        out_specs=[fx(C_WIDTH), bx(C_WIDTH)],
        out_shape=[jax.ShapeDtypeStruct((bsz, n, C_WIDTH), F32)] * 2,
        scratch_shapes=[pltpu.VMEM((C_HEADS, C_HEAD, C_STATE), F32)] * 2,
        compiler_params=_cp("parallel", "arbitrary"),
    )(xact, small, small_t, xact, small, small_t, prm, prm.T)


def _ssd_post_kernel(yf_ref, yb_ref, xs_ref, z_ref, dsk_ref, ng_ref, o_ref):
    y = yf_ref[...] + yb_ref[...] + xs_ref[...] * dsk_ref[...]
    y = y * _silu(z_ref[...])
    gw = C_WIDTH // C_GROUPS
    for g in range(C_GROUPS):
        blk = y[:, g * gw:(g + 1) * gw]
        ms = jnp.mean(blk * blk, axis=-1, keepdims=True)
        o_ref[:, g * gw:(g + 1) * gw] = blk * lax.rsqrt(ms + EPS) * ng_ref[:, g * gw:(g + 1) * gw]


def _ssd_post(yf, yb, xact, z, d_skip, norm_g):
    bsz, n, _ = yf.shape
    tok = pl.BlockSpec((None, TM, C_WIDTH), lambda b, i: (b, i, 0))
    return pl.pallas_call(
        _ssd_post_kernel,
        grid=(bsz, n // TM),
        in_specs=[tok, tok, tok, tok, _const_spec((1, C_WIDTH)), _const_spec((1, C_WIDTH))],
        out_specs=tok,
        out_shape=jax.ShapeDtypeStruct((bsz, n, C_WIDTH), F32),
        compiler_params=_cp("parallel", "parallel"),
    )(yf, yb, xact, z, jnp.repeat(d_skip, C_HEAD).reshape(1, C_WIDTH), norm_g.reshape(1, C_WIDTH))


def _gla_dir(qk, v, sm, up, bias, st_scr, y_ref, causal):
    length = qk.shape[0]
    hk = D_HEADS * D_KEY
    gd = sm[:, 2 * C_HEADS:2 * C_HEADS + D_LORA]
    zz = jnp.dot(gd, up, precision=HI, preferred_element_type=F32) + bias
    lg = -_softplus(-zz) * (1.0 / D_TAU)
    tri, _ = _scan_masks(length, causal)
    cum = jnp.dot(tri.astype(F32), lg, precision=HI, preferred_element_type=F32)
    last = length - 1 if causal else 0
    cl = cum[last:last + 1, :]
    q = qk[:, :hk] * (D_KEY ** -0.5)
    k = qk[:, hk:]
    qe = (q * jnp.exp(cum)).astype(BF16)
    ke = (k * jnp.exp(-cum)).astype(BF16)
    kd = (k * jnp.exp(cl - cum)).astype(BF16)
    el = jnp.exp(cl)
    for h in range(D_HEADS):
        sl = slice(h * D_KEY, (h + 1) * D_KEY)
        att = jnp.where(tri, _nt_dot(qe[:, sl], ke[:, sl]), 0.0)
        vh = v[:, h * D_VAL:(h + 1) * D_VAL]
        st = st_scr[h]
        y = jnp.dot(att.astype(BF16), vh.astype(BF16), preferred_element_type=F32)
        y_ref[:, h * D_VAL:(h + 1) * D_VAL] = y + _nt_dot(qe[:, sl], st.astype(BF16))
        st_scr[h] = st * el[:, sl] + jnp.dot(vh.T.astype(BF16), kd[:, sl], preferred_element_type=F32)


def _gla_scan_kernel(qkf_ref, vf_ref, smf_ref, qkb_ref, vb_ref, smb_ref, upf_ref, upb_ref, bias_ref,
                     yf_ref, yb_ref, sf_scr, sb_scr):
    @pl.when(pl.program_id(1) == 0)
    def _():
        sf_scr[...] = jnp.zeros_like(sf_scr)
        sb_scr[...] = jnp.zeros_like(sb_scr)

    _gla_dir(qkf_ref[...], vf_ref[...], smf_ref[...], upf_ref[...], bias_ref[0:1, :], sf_scr, yf_ref, True)
    _gla_dir(qkb_ref[...], vb_ref[...], smb_ref[...], upb_ref[...], bias_ref[1:2, :], sb_scr, yb_ref, False)


def _gla_scan(qk, v, small, gk_up_f, gk_b_f, gk_up_b, gk_b_b, n_ctx):
    bsz, n, _ = qk.shape
    L = D_CHUNK
    nc, ncc = n // L, n_ctx // L
    bw = _bwd_chunk(ncc, nc)
    hk = D_HEADS * D_KEY
    bias = jnp.zeros((8, hk), F32).at[0].set(gk_b_f).at[1].set(gk_b_b)
    fx = lambda w: pl.BlockSpec((None, L, w), lambda b, s: (b, s, 0))
    bx = lambda w: pl.BlockSpec((None, L, w), lambda b, s: (b, bw(s), 0))
    return pl.pallas_call(
        _gla_scan_kernel,
        grid=(bsz, nc),
        in_specs=[fx(2 * hk), fx(D_WIDTH), fx(128), bx(2 * hk), bx(D_WIDTH), bx(128),
                  _const_spec((D_LORA, hk)), _const_spec((D_LORA, hk)), _const_spec((8, hk))],
        out_specs=[fx(D_WIDTH), bx(D_WIDTH)],
        out_shape=[jax.ShapeDtypeStruct((bsz, n, D_WIDTH), F32)] * 2,
        scratch_shapes=[pltpu.VMEM((D_HEADS, D_VAL, D_KEY), F32)] * 2,
        compiler_params=_cp("parallel", "arbitrary"),
    )(qk, v, small, qk, v, small, gk_up_f, gk_up_b, bias)


def _gla_post_kernel(yf_ref, yb_ref, g_ref, ng_ref, o_ref):
    y = yf_ref[...] + yb_ref[...]
    gate = _silu(g_ref[...])
    for h in range(D_HEADS):
        sl = slice(h * D_VAL, (h + 1) * D_VAL)
        blk = y[:, sl]
        ms = jnp.mean(blk * blk, axis=-1, keepdims=True)
        o_ref[:, sl] = blk * lax.rsqrt(ms + EPS) * ng_ref[...] * gate[:, sl]


def _gla_post(yf, yb, g, norm_g):
    bsz, n, _ = yf.shape
    tok = pl.BlockSpec((None, TM, D_WIDTH), lambda b, i: (b, i, 0))
    return pl.pallas_call(
        _gla_post_kernel,
        grid=(bsz, n // TM),
        in_specs=[tok, tok, tok, _const_spec((1, D_VAL))],
        out_specs=tok,
        out_shape=jax.ShapeDtypeStruct((bsz, n, D_WIDTH), F32),
        compiler_params=_cp("parallel", "parallel"),
    )(yf, yb, g, norm_g.reshape(1, D_VAL))


def kernel(x, c, ctx, c_ctx, l0_mod_w, l0_mod_b, l0_norm1, l0_norm2, l0_w_in, l0_w_out, l0_mlp_w1, l0_mlp_w2, l0_lam_q1, l0_lam_k1, l0_lam_q2, l0_lam_k2, l0_subln, l0_mu, l0_w0_f, l0_w2_f, l0_w0_b, l0_w2_b, l0_a0, l0_a2, l0_g2, l0_k_k, l0_k_a, l0_r_k, l0_lnx_w, l0_lnx_b, l1_mod_w, l1_mod_b, l1_norm1, l1_norm2, l1_w_in, l1_w_out, l1_mlp_w1, l1_mlp_w2, l1_conv_w, l1_conv_b, l1_dt_bias_f, l1_a_log_f, l1_dt_bias_b, l1_a_log_b, l1_d_skip, l1_ssm_norm, l1_gk_up_f, l1_gk_b_f, l1_gk_up_b, l1_gk_b_b, l1_gla_norm, final_norm):
    bsz, n_lat, d = x.shape
    n_ctx = ctx.shape[1]
    assert d == D and n_ctx % TM == 0 and n_lat % TM == 0 and n_lat % GRID_W == 0
    nct = n_ctx // TM
    xc = jnp.concatenate([ctx, x], axis=1)

    mod0 = _modulation(c, c_ctx, l0_mod_w, l0_mod_b)
    qkv, pb = _inproj0(xc, mod0, l0_norm1, l0_w_in, _rope_tables(n_ctx, n_lat), nct)
    lamv = jnp.stack([l0_lam_q1, l0_lam_k1, l0_lam_q2, l0_lam_k2], axis=0)
    oa = _attention(qkv, lamv, l0_subln, 0, n_ctx)
    ob = _rwkv_mix(pb, n_ctx, l0_mu, l0_w0_f, l0_w2_f, l0_w0_b, l0_w2_b, l0_a0, l0_a2, l0_g2,
                   l0_k_k, l0_k_a, l0_r_k, l0_lnx_w, l0_lnx_b)
    xc = _post(xc, oa, ob, mod0, l0_w_out, l0_norm2, l0_mlp_w1, l0_mlp_w2, None, nct, False)

    mod1 = _modulation(c, c_ctx, l1_mod_w, l1_mod_b)
    z, xbc, qk, v, gg, small = _inproj1(xc, mod1, l1_norm1, l1_w_in, nct)
    xact = _conv(xbc, l1_conv_w, l1_conv_b, nct)
    yf_c, yb_c = _ssd_scan(xact, small, jnp.swapaxes(small, 1, 2), l1_dt_bias_f, l1_a_log_f,
                           l1_dt_bias_b, l1_a_log_b, n_ctx)
    oc = _ssd_post(yf_c, yb_c, xact, z, l1_d_skip, l1_ssm_norm)
    yf_d, yb_d = _gla_scan(qk, v, small, l1_gk_up_f, l1_gk_b_f, l1_gk_up_b, l1_gk_b_b, n_ctx)
    od = _gla_post(yf_d, yb_d, gg, l1_gla_norm)
    return _post(xc, oc, od, mod1, l1_w_out, l1_norm2, l1_mlp_w1, l1_mlp_w2, final_norm, nct, True)
```

```python
import functools
import math

import jax
import jax.numpy as jnp
from jax import lax
from jax.experimental import pallas as pl
from jax.experimental.pallas import tpu as pltpu

F32 = jnp.float32
BF16 = jnp.bfloat16
HI = lax.Precision.HIGHEST

D = 1024
EPS = 1e-6
GRID_W = 64
ROPE_THETA = 10000.0
MLP_HIDDEN = 4 * D

A_HEADS, A_QK, A_V = 4, 64, 128
A_PROJ = 1536
A_SUBLN_EPS = 1e-5
B_HEADS, B_HEAD = 8, 64
B_WIDTH = 512
B_PROJ = 1792
B_GN_EPS = 64e-5
C_HEADS, C_HEAD, C_GROUPS, C_REP, C_STATE = 8, 64, 2, 4, 128
C_WIDTH = 512
C_CONV = 5
C_CHUNK = 128
C_CONV_DIM = 1024
D_HEADS, D_KEY, D_VAL = 4, 64, 128
D_WIDTH = 512
D_LORA = 16
D_TAU = 16.0
D_CHUNK = 64
ODD_PAD = 3200

TM = 256
TQ = 128
B_CHUNK = 64
ATTN_KC = 512
GLA_NB = 4
HALO = 8
VMEM_LIMIT = 56 * 1024 * 1024


def _cp(*sem):
    return pltpu.CompilerParams(dimension_semantics=sem, vmem_limit_bytes=VMEM_LIMIT)


def _const_spec(shape):
    nd = len(shape)
    return pl.BlockSpec(shape, lambda *_: (0,) * nd, pipeline_mode=pl.Buffered(1))


def _softplus(x):
    return jnp.maximum(x, 0.0) + jnp.log(1.0 + jnp.exp(-jnp.abs(x)))


def _sigmoid(x):
    return 1.0 / (1.0 + jnp.exp(-x))


def _silu(x):
    return x * _sigmoid(x)


def _ada(x, g, sh, sc):
    ms = jnp.mean(x * x, axis=-1, keepdims=True)
    return (x * lax.rsqrt(ms + EPS) * g) * (1.0 + sc) + sh


def _segsum(x, e):
    hi = x.astype(BF16)
    r1 = x - hi.astype(F32)
    mid = r1.astype(BF16)
    lo = (r1 - mid.astype(F32)).astype(BF16)
    d = lambda a: jnp.dot(a, e, preferred_element_type=F32)
    return d(hi) + d(mid) + d(lo)


def _block_ones(n, seg):
    i = jnp.arange(n) // seg
    return (i[:, None] == i[None, :]).astype(BF16)


def _tile_is_lat(nct):
    return lambda b, i: (b, (i >= nct).astype(jnp.int32), 0, 0)


def _mod_kernel(c_ref, w_ref, b_ref, o_ref):
    s = _silu(c_ref[...])
    o_ref[...] = jnp.dot(s, w_ref[...], precision=HI, preferred_element_type=F32) + b_ref[...]


def _modulation(c, c_ctx, mod_w, mod_b):
    bsz = c.shape[0]
    rows = 8 * ((bsz + 1 + 7) // 8)
    cs = jnp.zeros((rows, D), F32).at[0].set(c_ctx).at[1:bsz + 1].set(c)
    n6 = mod_w.shape[1]
    tn = 1024
    m = pl.pallas_call(
        _mod_kernel,
        grid=(n6 // tn,),
        in_specs=[pl.BlockSpec((rows, D), lambda j: (0, 0)),
                  pl.BlockSpec((D, tn), lambda j: (0, j)),
                  pl.BlockSpec((1, tn), lambda j: (0, j))],
        out_specs=pl.BlockSpec((rows, tn), lambda j: (0, j)),
        out_shape=jax.ShapeDtypeStruct((rows, n6), F32),
        compiler_params=_cp("arbitrary"),
    )(cs, mod_w, mod_b.reshape(1, n6))
    m_c = jnp.broadcast_to(m[0:1], (bsz, n6))
    return jnp.stack([m_c, m[1:bsz + 1]], axis=1)[:, :, None, :]


def _rope_tables(n_ctx, n_lat):
    t = jnp.arange(n_lat)
    row = (t // GRID_W).astype(F32)
    col = (t % GRID_W).astype(F32)
    nf = A_QK // 4
    inv = ROPE_THETA ** (-jnp.arange(nf, dtype=F32) / nf)
    ar = row[:, None] * inv[None, :]
    ac = col[:, None] * inv[None, :]
    z = jnp.zeros_like(ar)
    c64 = jnp.concatenate([jnp.cos(ar), jnp.cos(ar), jnp.cos(ac), jnp.cos(ac)], axis=-1)
    s1 = jnp.concatenate([-jnp.sin(ar), z, -jnp.sin(ac), z], axis=-1)
    s2 = jnp.concatenate([z, jnp.sin(ar), z, jnp.sin(ac)], axis=-1)
    pad = lambda a, v: jnp.concatenate([jnp.full((n_ctx, 128), v, F32), jnp.tile(a, (1, 2))], axis=0)
    return pad(c64, 1.0), pad(s1, 0.0), pad(s2, 0.0)


def _inproj0_kernel(x_ref, mod_ref, g_ref, wa_ref, wb_ref, c_ref, s1_ref, s2_ref, qkv_ref, pb_ref):
    mod = mod_ref[...]
    h = _ada(x_ref[...], g_ref[...], mod[:, 0:D], mod[:, D:2 * D]).astype(BF16)
    pa = jnp.dot(h, wa_ref[...], preferred_element_type=F32)
    c, s1, s2 = c_ref[...], s1_ref[...], s2_ref[...]
    scale = A_QK ** -0.5
    for j in range(8):
        blk = pa[:, j * 128:(j + 1) * 128]
        rot = blk * c + pltpu.roll(blk, 112, 1) * s1 + pltpu.roll(blk, 16, 1) * s2
        if j < 4:
            rot = rot * scale
        qkv_ref[:, j * 128:(j + 1) * 128] = rot.astype(BF16)
    qkv_ref[:, 1024:1536] = pa[:, 1024:1536].astype(BF16)
    pb_ref[...] = jnp.dot(h, wb_ref[...], preferred_element_type=F32)


def _inproj0(xc, mod, norm1, w_in, tabs, nct):
    bsz, n, _ = xc.shape
    wa = w_in[:, :A_PROJ].astype(BF16)
    wb = w_in[:, A_PROJ:].astype(BF16)
    tok = lambda w: pl.BlockSpec((None, TM, w), lambda b, i: (b, i, 0))
    tab = pl.BlockSpec((TM, 128), lambda b, i: (i, 0))
    return pl.pallas_call(
        _inproj0_kernel,
        grid=(bsz, n // TM),
        in_specs=[tok(D), pl.BlockSpec((None, None, 1, 6 * D), _tile_is_lat(nct)),
                  _const_spec((1, D)), _const_spec((D, A_PROJ)), _const_spec((D, B_PROJ)),
                  tab, tab, tab],
        out_specs=[tok(A_PROJ), tok(B_PROJ)],
        out_shape=[jax.ShapeDtypeStruct((bsz, n, A_PROJ), BF16),
                   jax.ShapeDtypeStruct((bsz, n, B_PROJ), F32)],
        compiler_params=_cp("parallel", "parallel"),
    )(xc, mod, norm1.reshape(1, D), wa, wb, *tabs)


def _attn_kernel(lam_ref, q_ref, k_ref, v_ref, subln_ref, o_ref, *, lam_init, n_ctx, nq_ctx):
    qi = pl.program_id(2)
    lv = lam_ref[...]
    lam = (jnp.exp(jnp.sum(lv[0:1] * lv[1:2], axis=-1, keepdims=True))
           - jnp.exp(jnp.sum(lv[2:3] * lv[3:4], axis=-1, keepdims=True)) + lam_init)
    q = q_ref[...]
    tq = q.shape[0]
    lane = lax.broadcasted_iota(jnp.int32, q.shape, 1)
    zero = jnp.zeros_like(q)
    q2 = jnp.concatenate([jnp.where(lane < A_QK, q, zero), jnp.where(lane >= A_QK, q, zero)], axis=0)

    def attend(chunks):
        scores = lambda c: _nt_dot(q2, k_ref[c[0]:c[0] + c[1], :])
        m = l = acc = None
        s_next = scores(chunks[0])
        for ci, c in enumerate(chunks):
            s = s_next
            if ci + 1 < len(chunks):
                s_next = scores(chunks[ci + 1])
            v = v_ref[c[0]:c[0] + c[1], :]
            mc = jnp.max(s, axis=-1, keepdims=True)
            if m is None:
                m = mc
                p = jnp.exp(s - m)
                l = jnp.sum(p, axis=-1, keepdims=True)
                acc = jnp.dot(p.astype(BF16), v, preferred_element_type=F32)
            else:
                m_new = jnp.maximum(m, mc)
                alpha = jnp.exp(m - m_new)
                p = jnp.exp(s - m_new)
                l = alpha * l + jnp.sum(p, axis=-1, keepdims=True)
                acc = alpha * acc + jnp.dot(p.astype(BF16), v, preferred_element_type=F32)
                m = m_new
        on = acc / l
        o = on[:tq] - lam * on[tq:]
        ms = jnp.mean(o * o, axis=-1, keepdims=True)
        o_ref[...] = (o * lax.rsqrt(ms + A_SUBLN_EPS) * subln_ref[...]) * (1.0 - lam_init)

    n_all = k_ref.shape[0]
    kc = min(ATTN_KC, n_all - n_ctx)
    ctx_chunks = [(0, n_ctx)]
    lat_chunks = [(n_ctx + j * kc, kc) for j in range((n_all - n_ctx) // kc)]

    @pl.when(qi < nq_ctx)
    def _():
        attend(ctx_chunks)

    @pl.when(qi >= nq_ctx)
    def _():
        attend(ctx_chunks + lat_chunks)


def _attention(qkv, lamv, subln, layer_idx, n_ctx):
    bsz, n, _ = qkv.shape
    lam_init = 0.8 - 0.6 * math.exp(-0.3 * layer_idx)
    kern = functools.partial(_attn_kernel, lam_init=lam_init, n_ctx=n_ctx, nq_ctx=n_ctx // TQ)
    return pl.pallas_call(
        kern,
        grid=(bsz, A_HEADS, n // TQ),
        in_specs=[_const_spec((4, A_QK)),
                  pl.BlockSpec((None, TQ, 128), lambda b, h, i: (b, i, h)),
                  pl.BlockSpec((None, n, 128), lambda b, h, i: (b, 0, A_HEADS + h)),
                  pl.BlockSpec((None, n, 128), lambda b, h, i: (b, 0, 2 * A_HEADS + h)),
                  _const_spec((1, A_V))],
        out_specs=pl.BlockSpec((None, TQ, 128), lambda b, h, i: (b, i, h)),
        out_shape=jax.ShapeDtypeStruct((bsz, n, A_HEADS * A_V), F32),
        compiler_params=_cp("parallel", "parallel", "arbitrary"),
    )(lamv, qkv, qkv, qkv, subln.reshape(1, A_V))


def _halo_flags(i, nct, nt):
    prev_ok = jnp.logical_and(i != 0, i != nct)
    next_ok = jnp.logical_and(i != nct - 1, i != nt - 1)
    return prev_ok, next_ok


def _rwkv_prep_kernel(p_ref, pp_ref, pn_ref, mu_ref, vec_ref, w2f_ref, w2b_ref, a2_ref, g2_ref, e_ref,
                      r_ref, wf_ref, wb_ref, k_ref, v_ref, kk_ref, kb_ref, bonus_ref, g_ref, *, nct, nt):
    i = pl.program_id(1)
    f = p_ref[...]
    tm = f.shape[0]
    prev_ok, next_ok = _halo_flags(i, nct, nt)
    hp = jnp.where(prev_ok, pp_ref[HALO - 1:HALO, :], 0.0)
    hn = jnp.where(next_ok, pn_ref[0:1, :], 0.0)
    row = lax.broadcasted_iota(jnp.int32, f.shape, 0)
    prev = jnp.where(row == 0, hp, pltpu.roll(f, 1, 0))
    nxt = jnp.where(row == tm - 1, hn, pltpu.roll(f, tm - 1, 0))
    fm = f + (0.5 * (prev + nxt) - f) * mu_ref[...]
    r = fm[:, 0:512]
    k = fm[:, 512:1024]
    v = fm[:, 1024:1536]
    wd = fm[:, 1536:1600]
    ad = fm[:, 1600:1664]
    gd = fm[:, 1664:1792]
    w0f, w0b, a0 = vec_ref[0:1, :], vec_ref[1:2, :], vec_ref[2:3, :]
    k_k, k_a, r_k = vec_ref[3:4, :], vec_ref[4:5, :], vec_ref[5:6, :]
    e = e_ref[...]
    mm = lambda a, w_ref: jnp.dot(a.astype(BF16), w_ref[...], preferred_element_type=F32)
    a = _sigmoid(a0 + mm(ad, a2_ref))
    tw = jnp.tanh(wd)
    decay = lambda w0, w2_ref: -jnp.exp(-_softplus(-(w0 + mm(tw, w2_ref))) - 0.5)
    kkr = k * k_k
    kk = kkr / jnp.maximum(jnp.sqrt(_segsum(kkr * kkr, e)), 1e-12)
    k2 = k * (1.0 + (a - 1.0) * k_a)
    r_ref[...] = r
    wf_ref[...] = decay(w0f, w2f_ref)
    wb_ref[...] = decay(w0b, w2b_ref)
    k_ref[...] = k2
    v_ref[...] = v
    kk_ref[...] = kk
    kb_ref[...] = kk * a
    bonus_ref[...] = _segsum(r * k2 * r_k, e) * v
    g_ref[...] = mm(_sigmoid(gd), g2_ref)


def _rwkv_prep(pb, mu, vecs, w2_f, w2_b, a2, g2, nct):
    bsz, n, _ = pb.shape
    nt = n // TM
    hb = TM // HALO
    tok = lambda w: pl.BlockSpec((None, TM, w), lambda b, i: (b, i, 0))
    kern = functools.partial(_rwkv_prep_kernel, nct=nct, nt=nt)
    outs = pl.pallas_call(
        kern,
        grid=(bsz, nt),
        in_specs=[tok(B_PROJ),
                  pl.BlockSpec((None, HALO, B_PROJ), lambda b, i: (b, jnp.maximum(i * hb - 1, 0), 0)),
                  pl.BlockSpec((None, HALO, B_PROJ), lambda b, i: (b, jnp.minimum((i + 1) * hb, nt * hb - 1), 0)),
                  _const_spec((1, B_PROJ)), _const_spec((8, B_WIDTH)),
                  _const_spec((64, B_WIDTH)), _const_spec((64, B_WIDTH)), _const_spec((64, B_WIDTH)),
                  _const_spec((128, B_WIDTH)), _const_spec((B_WIDTH, B_WIDTH))],
        out_specs=[tok(B_WIDTH)] * 9,
        out_shape=[jax.ShapeDtypeStruct((bsz, n, B_WIDTH), F32)] * 9,
        compiler_params=_cp("parallel", "parallel"),
    )(pb, pb, pb, mu.reshape(1, B_PROJ), vecs, w2_f.astype(BF16), w2_b.astype(BF16), a2.astype(BF16),
      g2.astype(BF16), _block_ones(B_WIDTH, B_HEAD))
    return outs


def _cumsum3(tri_b, x):
    hi = x.astype(BF16)
    r1 = x - hi.astype(F32)
    mid = r1.astype(BF16)
    lo = (r1 - mid.astype(F32)).astype(BF16)
    d = lambda a: jnp.dot(tri_b, a, preferred_element_type=F32)
    return d(hi) + d(mid) + d(lo)


def _head_spread(heads, width):
    row = lax.broadcasted_iota(jnp.int32, (heads, heads * width), 0)
    col = lax.broadcasted_iota(jnp.int32, (heads, heads * width), 1)
    return (row == col // width).astype(BF16)


def _bdot(a, b):
    return jnp.dot(a.astype(BF16), b.astype(BF16), preferred_element_type=F32)


def _rwkv_dir(r, lw, k, v, kk, kb, s_scr, y_ref, causal):
    length = r.shape[0]
    tri, _ = _scan_masks(length, causal)
    ii = lax.broadcasted_iota(jnp.int32, (length, length), 0)
    jj = lax.broadcasted_iota(jnp.int32, (length, length), 1)
    strict = jnp.logical_and(tri, ii != jj)
    lvl = []
    m = 1
    while m < length:
        same2 = (ii // (2 * m)) == (jj // (2 * m))
        diff1 = (ii // m) != (jj // m)
        lvl.append(jnp.logical_and(jnp.logical_and(same2, diff1), strict))
        m *= 2
    cum = _cumsum3(tri.astype(BF16), lw)
    last = length - 1 if causal else 0
    cl = cum[last:last + 1, :]
    gi = jnp.exp(-cum)
    gl = jnp.exp(cl - cum)
    kg = kk * jnp.exp(cum - lw)
    rg = r * jnp.exp(cum)
    ki, bi = k * gi, kb * gi
    kd, bd = k * gl, kb * gl
    el = jnp.exp(cl)
    probs = []
    for h in range(B_HEADS):
        sl = slice(h * B_HEAD, (h + 1) * B_HEAD)
        probs.append(dict(
            sl=sl, tri=tri, strict=strict, lvl=lvl, s_scr=s_scr, y_ref=y_ref, h=h,
            lhs=jnp.concatenate([kg[:, sl], rg[:, sl]], axis=0).astype(BF16),
            bi=bi[:, sl].astype(BF16), ki=ki[:, sl].astype(BF16), vh=v[:, sl],
            kbd=jnp.concatenate([kd[:, sl], -bd[:, sl]], axis=0).astype(BF16),
            el=el[:, sl]))
    return probs


def _rwkv_solve(probs):
    length = probs[0]["vh"].shape[0]
    for p in probs:
        p["sc_b"] = _nt_dot(p["lhs"], p["bi"])
        p["sc_k"] = _nt_dot(p["lhs"], p["ki"])
    for p in probs:
        p["a_kb"] = jnp.where(p["strict"], p["sc_b"][:length], 0.0)
        p["a_kk"] = jnp.where(p["strict"], p["sc_k"][:length], 0.0).astype(BF16)
        p["a_rb"] = jnp.where(p["tri"], p["sc_b"][length:], 0.0).astype(BF16)
        p["a_rk"] = jnp.where(p["tri"], p["sc_k"][length:], 0.0).astype(BF16)
        p["nn"] = -jnp.where(p["lvl"][0], p["a_kb"], 0.0)
        p["s0"] = p["s_scr"][p["h"]]
    for p in probs:
        p["hs"] = _nt_dot(p["lhs"], p["s0"].astype(BF16))
        p["akv"] = _bdot(p["a_kk"], p["vh"])
    for lv in range(1, len(probs[0]["lvl"])):
        for p in probs:
            aoff = jnp.where(p["lvl"][lv], p["a_kb"], 0.0)
            p["x"] = aoff + _bdot(p["nn"], aoff)
        for p in probs:
            p["nn"] = p["nn"] - (p["x"] + _bdot(p["x"], p["nn"]))
    for p in probs:
        rhs = p["hs"][:length] + p["akv"]
        p["u"] = rhs + _bdot(p["nn"], rhs)
    for p in probs:
        p["y_ref"][:, p["sl"]] = p["hs"][length:] + _bdot(p["a_rk"], p["vh"]) - _bdot(p["a_rb"], p["u"])
    for p in probs:
        vu = jnp.concatenate([p["vh"], p["u"]], axis=0)
        p["s_scr"][p["h"]] = p["s0"] * p["el"] + _bdot(vu.T, p["kbd"])


def _rwkv_chunk_kernel(*refs):
    fwd, bwd = refs[0:6], refs[6:12]
    yf_ref, yb_ref, sf_scr, sb_scr = refs[12:16]

    @pl.when(pl.program_id(1) == 0)
    def _():
        sf_scr[...] = jnp.zeros_like(sf_scr)
        sb_scr[...] = jnp.zeros_like(sb_scr)

    probs = _rwkv_dir(*[x[...] for x in fwd], sf_scr, yf_ref, True)
    probs += _rwkv_dir(*[x[...] for x in bwd], sb_scr, yb_ref, False)
    _rwkv_solve(probs)


def _rwkv_chunk_scan(r, lwf, lwb, k, v, kk, kb, n_ctx):
    bsz, n, width = r.shape
    L = B_CHUNK
    nc, ncc = n // L, n_ctx // L
    bw = _bwd_chunk(ncc, nc)
    fx = pl.BlockSpec((None, L, width), lambda b, s: (b, s, 0))
    bx = pl.BlockSpec((None, L, width), lambda b, s: (b, bw(s), 0))
    return pl.pallas_call(
        _rwkv_chunk_kernel,
        grid=(bsz, nc),
        in_specs=[fx] * 6 + [bx] * 6,
        out_specs=[fx, bx],
        out_shape=[jax.ShapeDtypeStruct((bsz, n, width), F32)] * 2,
        scratch_shapes=[pltpu.VMEM((B_HEADS, B_HEAD, B_HEAD), F32)] * 2,
        compiler_params=_cp("parallel", "arbitrary"),
    )(r, lwf, k, v, kk, kb, r, lwb, k, v, kk, kb)


def _rwkv_post_kernel(yf_ref, yb_ref, bonus_ref, g_ref, lnw_ref, lnb_ref, e_ref, o_ref):
    e = e_ref[...]
    y = yf_ref[...] + yb_ref[...]
    d = y - _segsum(y, e) * (1.0 / B_HEAD)
    var = _segsum(d * d, e) * (1.0 / B_HEAD)
    yn = d * lax.rsqrt(var + B_GN_EPS) * lnw_ref[...] + lnb_ref[...]
    o_ref[...] = (yn + bonus_ref[...]) * g_ref[...]


def _rwkv_post(yf, yb, bonus, g, lnx_w, lnx_b):
    bsz, n, _ = yf.shape
    tok = pl.BlockSpec((None, TM, B_WIDTH), lambda b, i: (b, i, 0))
    return pl.pallas_call(
        _rwkv_post_kernel,
        grid=(bsz, n // TM),
        in_specs=[tok] * 4 + [_const_spec((1, B_WIDTH)), _const_spec((1, B_WIDTH)),
                              _const_spec((B_WIDTH, B_WIDTH))],
        out_specs=tok,
        out_shape=jax.ShapeDtypeStruct((bsz, n, B_WIDTH), F32),
        compiler_params=_cp("parallel", "parallel"),
    )(yf, yb, bonus, g, lnx_w.reshape(1, -1), lnx_b.reshape(1, -1), _block_ones(B_WIDTH, B_HEAD))


def _rwkv_mix(pb, n_ctx, mu, w0_f, w2_f, w0_b, w2_b, a0, a2, g2, k_k, k_a, r_k, lnx_w, lnx_b):
    zero = jnp.zeros_like(w0_f)
    vecs = jnp.stack([w0_f, w0_b, a0, k_k, k_a, r_k, zero, zero], axis=0)
    r, lwf, lwb, k2, v, kk, kb, bonus, g = _rwkv_prep(pb, mu, vecs, w2_f, w2_b, a2, g2, n_ctx // TM)
    yf, yb = _rwkv_chunk_scan(r, lwf, lwb, k2, v, kk, kb, n_ctx)
    return _rwkv_post(yf, yb, bonus, g, lnx_w, lnx_b)


def _post_kernel(x_ref, o1_ref, o2_ref, mod_ref, woa_ref, wob_ref, n2_ref, w1_ref, w2_ref, fn_ref, out_ref,
                 *, final):
    mod = mod_ref[...]
    g1, sh2 = mod[:, 2 * D:3 * D], mod[:, 3 * D:4 * D]
    sc2, g2 = mod[:, 4 * D:5 * D], mod[:, 5 * D:6 * D]
    mix = (jnp.dot(o1_ref[...].astype(BF16), woa_ref[...], preferred_element_type=F32)
           + jnp.dot(o2_ref[...].astype(BF16), wob_ref[...], preferred_element_type=F32))
    x1 = x_ref[...] + g1 * mix
    h = _ada(x1, n2_ref[...], sh2, sc2).astype(BF16)
    hid = jnp.maximum(jnp.dot(h, w1_ref[...], preferred_element_type=F32), 0.0)
    hid = (hid * hid).astype(BF16)
    x2 = x1 + g2 * jnp.dot(hid, w2_ref[...], preferred_element_type=F32)
    if final:
        ms = jnp.mean(x2 * x2, axis=-1, keepdims=True)
        x2 = x2 * lax.rsqrt(ms + EPS) * fn_ref[...]
    out_ref[...] = x2


def _post(xc, o1, o2, mod, w_out, norm2, mlp_w1, mlp_w2, final_norm, nct, lat_only):
    bsz, n, _ = xc.shape
    half = o1.shape[-1]
    t0 = nct if lat_only else 0
    nt = n // TM - t0
    tok = lambda w: pl.BlockSpec((None, TM, w), lambda b, i: (b, i + t0, 0))
    final = final_norm is not None
    fn = (final_norm if final else jnp.ones((D,), F32)).reshape(1, D)
    kern = functools.partial(_post_kernel, final=final)
    return pl.pallas_call(
        kern,
        grid=(bsz, nt),
        in_specs=[tok(D), tok(half), tok(half),
                  pl.BlockSpec((None, None, 1, 6 * D), lambda b, i: (b, (i + t0 >= nct).astype(jnp.int32), 0, 0)),
                  _const_spec((half, D)), _const_spec((half, D)), _const_spec((1, D)),
                  _const_spec((D, MLP_HIDDEN)), _const_spec((MLP_HIDDEN, D)), _const_spec((1, D))],
        out_specs=pl.BlockSpec((None, TM, D), lambda b, i: (b, i, 0)),
        out_shape=jax.ShapeDtypeStruct((bsz, nt * TM, D), F32),
        compiler_params=_cp("parallel", "parallel"),
    )(xc, o1, o2, mod, w_out[:half].astype(BF16), w_out[half:].astype(BF16), norm2.reshape(1, D),
      mlp_w1.astype(BF16), mlp_w2.astype(BF16), fn)


def _inproj1_kernel(x_ref, mod_ref, g_ref, w_ref, z_ref, xbc_ref, qk_ref, v_ref, gg_ref, sm_ref):
    mod = mod_ref[...]
    h = _ada(x_ref[...], g_ref[...], mod[:, 0:D], mod[:, D:2 * D]).astype(BF16)
    p = jnp.dot(h, w_ref[...], preferred_element_type=F32)
    z_ref[...] = p[:, 0:512]
    xbc_ref[...] = p[:, 512:1536]
    qk_ref[...] = p[:, 1536:2048]
    v_ref[...] = p[:, 2048:2560]
    gg_ref[...] = p[:, 2560:3072]
    sm_ref[...] = p[:, 3072:3200]


def _inproj1(xc, mod, norm1, w_in, nct):
    bsz, n, _ = xc.shape
    c0 = C_WIDTH + C_CONV_DIM
    d0 = c0 + 2 * C_HEADS
    qk_w = 2 * D_HEADS * D_KEY
    w = jnp.concatenate([w_in[:, :c0], w_in[:, d0:d0 + qk_w + 2 * D_WIDTH], w_in[:, c0:d0],
                         w_in[:, d0 + qk_w + 2 * D_WIDTH:],
                         jnp.zeros((D, 128 - 2 * C_HEADS - D_LORA), F32)], axis=1).astype(BF16)
    tok = lambda wd: pl.BlockSpec((None, TM, wd), lambda b, i: (b, i, 0))
    widths = [512, 1024, 512, 512, 512, 128]
    return pl.pallas_call(
        _inproj1_kernel,
        grid=(bsz, n // TM),
        in_specs=[tok(D), pl.BlockSpec((None, None, 1, 6 * D), _tile_is_lat(nct)),
                  _const_spec((1, D)), _const_spec((D, ODD_PAD))],
        out_specs=[tok(wd) for wd in widths],
        out_shape=[jax.ShapeDtypeStruct((bsz, n, wd), F32) for wd in widths],
        compiler_params=_cp("parallel", "parallel"),
    )(xc, mod, norm1.reshape(1, D), w)


def _conv_kernel(x_ref, xp_ref, xn_ref, w_ref, b_ref, o_ref, *, nct, nt):
    i = pl.program_id(1)
    x = x_ref[...]
    tm = x.shape[0]
    prev_ok, next_ok = _halo_flags(i, nct, nt)
    hp = jnp.where(prev_ok, xp_ref[...], 0.0)
    hn = jnp.where(next_ok, xn_ref[...], 0.0)
    ext = jnp.concatenate([hp, x, hn], axis=0)
    ne = tm + 2 * HALO
    acc = b_ref[...] + w_ref[2:3, :] * x
    for j in (0, 1, 3, 4):
        shifted = pltpu.roll(ext, (C_CONV // 2 - j) % ne, 0)[HALO:HALO + tm]
        acc = acc + w_ref[j:j + 1, :] * shifted
    o_ref[...] = _silu(acc)


def _conv(xbc, conv_w, conv_b, nct):
    bsz, n, cdim = xbc.shape
    nt = n // TM
    hb = TM // HALO
    tok = pl.BlockSpec((None, TM, cdim), lambda b, i: (b, i, 0))
    w8 = jnp.zeros((8, cdim), F32).at[:C_CONV].set(conv_w)
    return pl.pallas_call(
        functools.partial(_conv_kernel, nct=nct, nt=nt),
        grid=(bsz, nt),
        in_specs=[tok,
                  pl.BlockSpec((None, HALO, cdim), lambda b, i: (b, jnp.maximum(i * hb - 1, 0), 0)),
                  pl.BlockSpec((None, HALO, cdim), lambda b, i: (b, jnp.minimum((i + 1) * hb, nt * hb - 1), 0)),
                  _const_spec((8, cdim)), _const_spec((1, cdim))],
        out_specs=tok,
        out_shape=jax.ShapeDtypeStruct((bsz, n, cdim), F32),
        compiler_params=_cp("parallel", "parallel"),
    )(xbc, xbc, xbc, w8, conv_b.reshape(1, cdim))


def _scan_masks(length, causal):
    ii = lax.broadcasted_iota(jnp.int32, (length, length), 0)
    jj = lax.broadcasted_iota(jnp.int32, (length, length), 1)
    tri = (ii >= jj) if causal else (ii <= jj)
    tri_t = (ii <= jj) if causal else (ii >= jj)
    return tri, tri_t


def _nt_dot(a, b):
    return lax.dot_general(a, b, (((1,), (1,)), ((), ())), preferred_element_type=F32)


def _ssd_dir(xa, sm, sm_t, col0, prm, prm_t, h_scr, y_ref, causal):
    length = xa.shape[0]
    dtb, alog = prm[0:1, col0:col0 + C_HEADS], prm[1:2, col0:col0 + C_HEADS]
    dtb_t, alog_t = prm_t[col0:col0 + C_HEADS, 0:1], prm_t[col0:col0 + C_HEADS, 1:2]
    dt = _softplus(sm[:, col0:col0 + C_HEADS] + dtb)
    la = dt * (-jnp.exp(alog))
    la_t = _softplus(sm_t[col0:col0 + C_HEADS, :] + dtb_t) * (-jnp.exp(alog_t))
    tri, tri_t = _scan_masks(length, causal)
    cum = jnp.dot(tri.astype(F32), la, precision=HI, preferred_element_type=F32)
    cum_t = jnp.dot(la_t, tri_t.astype(F32), precision=HI, preferred_element_type=F32)
    last = length - 1 if causal else 0
    ecl = jnp.exp(cum[last:last + 1, :])
    spread = lambda a, w: _segsum(a, _head_spread(C_HEADS, w))
    xdt_all = xa[:, :C_WIDTH] * spread(dt, C_HEAD)
    xw_all = xdt_all * spread(jnp.exp(cum[last:last + 1, :] - cum), C_HEAD)
    ech_all = spread(jnp.exp(cum), C_HEAD)
    cum_all = spread(cum, length)
    groups, probs = [], []
    for g in range(C_GROUPS):
        bg = xa[:, C_WIDTH + g * C_STATE:C_WIDTH + (g + 1) * C_STATE].astype(BF16)
        cg = xa[:, C_WIDTH + (C_GROUPS + g) * C_STATE:C_WIDTH + (C_GROUPS + g + 1) * C_STATE].astype(BF16)
        grp = dict(bg=bg, cg=cg)
        groups.append(grp)
        for rr in range(C_REP):
            h = g * C_REP + rr
            sl = slice(h * C_HEAD, (h + 1) * C_HEAD)
            probs.append(dict(
                grp=grp, h=h, h_scr=h_scr, tri=tri,
                diff=cum_all[:, h * length:(h + 1) * length] - cum_t[h:h + 1, :],
                xdt=xdt_all[:, sl].astype(BF16), xw=xw_all[:, sl], ech=ech_all[:, sl], ecl=ecl[:, h:h + 1]))
    return groups, probs, y_ref


def _ssd_solve(dirs):
    for groups, _, _ in dirs:
        for grp in groups:
            grp["cb"] = _nt_dot(grp["cg"], grp["bg"])
    probs = [p for _, ps, _ in dirs for p in ps]
    for p in probs:
        lmat = jnp.exp(jnp.where(p["tri"], p["diff"], -jnp.inf))
        p["m"] = (p["grp"]["cb"] * lmat).astype(BF16)
        p["hs"] = p["h_scr"][p["h"]]
    for p in probs:
        p["y"] = (jnp.dot(p["m"], p["xdt"], preferred_element_type=F32)
                  + _nt_dot(p["grp"]["cg"], p["hs"].astype(BF16)) * p["ech"])
    for p in probs:
        xw = p["xw"].T.astype(BF16)
        p["h_scr"][p["h"]] = p["hs"] * p["ecl"] + jnp.dot(xw, p["grp"]["bg"], preferred_element_type=F32)
    for _, ps, y_ref in dirs:
        y_ref[...] = jnp.concatenate([p["y"] for p in ps], axis=-1)


def _ssd_scan_kernel(xf_ref, smf_ref, smtf_ref, xb_ref, smb_ref, smtb_ref, prm_ref, prmt_ref,
                     yf_ref, yb_ref, hf_scr, hb_scr):
    @pl.when(pl.program_id(1) == 0)
    def _():
        hf_scr[...] = jnp.zeros_like(hf_scr)
        hb_scr[...] = jnp.zeros_like(hb_scr)

    prm, prm_t = prm_ref[...], prmt_ref[...]
    _ssd_solve([_ssd_dir(xf_ref[...], smf_ref[...], smtf_ref[...], 0, prm, prm_t, hf_scr, yf_ref, True),
                _ssd_dir(xb_ref[...], smb_ref[...], smtb_ref[...], C_HEADS, prm, prm_t, hb_scr, yb_ref, False)])


def _bwd_chunk(ncc, nc):
    return lambda s: jnp.where(s < ncc, ncc - 1 - s, nc - 1 - s + ncc)


def _ssd_scan(xact, small, small_t, dt_bias_f, a_log_f, dt_bias_b, a_log_b, n_ctx):
    bsz, n, cdim = xact.shape
    L = C_CHUNK
    nc, ncc = n // L, n_ctx // L
    bw = _bwd_chunk(ncc, nc)
    prm = jnp.zeros((8, 128), F32).at[0, :16].set(jnp.concatenate([dt_bias_f, dt_bias_b]))
    prm = prm.at[1, :16].set(jnp.concatenate([a_log_f, a_log_b]))
    fx = lambda w: pl.BlockSpec((None, L, w), lambda b, s: (b, s, 0))
    bx = lambda w: pl.BlockSpec((None, L, w), lambda b, s: (b, bw(s), 0))
    ft = pl.BlockSpec((None, 128, L), lambda b, s: (b, 0, s))
    bt = pl.BlockSpec((None, 128, L), lambda b, s: (b, 0, bw(s)))
    return pl.pallas_call(
        _ssd_scan_kernel,
        grid=(bsz, nc),
        in_specs=[fx(cdim), fx(128), ft, bx(cdim), bx(128), bt, _const_spec((8, 128)), _const_spec((128, 8))],
        out_specs=[fx(C_WIDTH), bx(C_WIDTH)],
        out_shape=[jax.ShapeDtypeStruct((bsz, n, C_WIDTH), F32)] * 2,
        scratch_shapes=[pltpu.VMEM((C_HEADS, C_HEAD, C_STATE), F32)] * 2,
        compiler_params=_cp("parallel", "arbitrary"),
    )(xact, small, small_t, xact, small, small_t, prm, prm.T)


def _ssd_post_kernel(yf_ref, yb_ref, xs_ref, z_ref, dsk_ref, ng_ref, o_ref):
    y = yf_ref[...] + yb_ref[...] + xs_ref[...] * dsk_ref[...]
    y = y * _silu(z_ref[...])
    gw = C_WIDTH // C_GROUPS
    for g in range(C_GROUPS):
        blk = y[:, g * gw:(g + 1) * gw]
        ms = jnp.mean(blk * blk, axis=-1, keepdims=True)
        o_ref[:, g * gw:(g + 1) * gw] = blk * lax.rsqrt(ms + EPS) * ng_ref[:, g * gw:(g + 1) * gw]


def _ssd_post(yf, yb, xact, z, d_skip, norm_g):
    bsz, n, _ = yf.shape
    tok = pl.BlockSpec((None, TM, C_WIDTH), lambda b, i: (b, i, 0))
    return pl.pallas_call(
        _ssd_post_kernel,
        grid=(bsz, n // TM),
        in_specs=[tok, tok, tok, tok, _const_spec((1, C_WIDTH)), _const_spec((1, C_WIDTH))],
        out_specs=tok,
        out_shape=jax.ShapeDtypeStruct((bsz, n, C_WIDTH), F32),
        compiler_params=_cp("parallel", "parallel"),
    )(yf, yb, xact, z, jnp.repeat(d_skip, C_HEAD).reshape(1, C_WIDTH), norm_g.reshape(1, C_WIDTH))


def _gla_solve(items):
    hk = D_HEADS * D_KEY
    for it in items:
        gd = it["sm"][:, 2 * C_HEADS:2 * C_HEADS + D_LORA]
        it["zz"] = jnp.dot(gd, it["up"], precision=HI, preferred_element_type=F32) + it["bias"]
    for it in items:
        lg = -_softplus(-it["zz"]) * (1.0 / D_TAU)
        it["tri"], _ = _scan_masks(lg.shape[0], it["causal"])
        it["cum"] = _cumsum3(it["tri"].astype(BF16), lg)
    probs = []
    for it in items:
        cum = it["cum"]
        last = cum.shape[0] - 1 if it["causal"] else 0
        cl = cum[last:last + 1, :]
        q = it["qk"][:, :hk] * (D_KEY ** -0.5)
        k = it["qk"][:, hk:]
        qe = (q * jnp.exp(cum)).astype(BF16)
        ke = (k * jnp.exp(-cum)).astype(BF16)
        kd = (k * jnp.exp(cl - cum)).astype(BF16)
        el = jnp.exp(cl)
        for h in range(D_HEADS):
            sl = slice(h * D_KEY, (h + 1) * D_KEY)
            probs.append(dict(it=it, h=h, qe=qe[:, sl], ke=ke[:, sl], kd=kd[:, sl], el=el[:, sl],
                              vh=it["v"][:, h * D_VAL:(h + 1) * D_VAL]))
    for p in probs:
        p["att"] = jnp.where(p["it"]["tri"], _nt_dot(p["qe"], p["ke"]), 0.0).astype(BF16)
        p["st"] = p["it"]["st_scr"][p["it"]["base"] + p["h"]]
    for p in probs:
        y = jnp.dot(p["att"], p["vh"].astype(BF16), preferred_element_type=F32)
        p["it"]["y_ref"][p["it"]["bi"], :, p["h"] * D_VAL:(p["h"] + 1) * D_VAL] = (
            y + _nt_dot(p["qe"], p["st"].astype(BF16)))
    for p in probs:
        p["it"]["st_scr"][p["it"]["base"] + p["h"]] = (
            p["st"] * p["el"] + jnp.dot(p["vh"].T.astype(BF16), p["kd"], preferred_element_type=F32))


def _gla_scan_kernel(qkf_ref, vf_ref, smf_ref, qkb_ref, vb_ref, smb_ref, upf_ref, upb_ref, bias_ref,
                     yf_ref, yb_ref, sf_scr, sb_scr):
    @pl.when(pl.program_id(1) == 0)
    def _():
        sf_scr[...] = jnp.zeros_like(sf_scr)
        sb_scr[...] = jnp.zeros_like(sb_scr)

    items = []
    for bi in range(qkf_ref.shape[0]):
        items.append(dict(qk=qkf_ref[bi], v=vf_ref[bi], sm=smf_ref[bi], up=upf_ref[...], bias=bias_ref[0:1, :],
                          causal=True, st_scr=sf_scr, base=bi * D_HEADS, y_ref=yf_ref, bi=bi))
        items.append(dict(qk=qkb_ref[bi], v=vb_ref[bi], sm=smb_ref[bi], up=upb_ref[...], bias=bias_ref[1:2, :],
                          causal=False, st_scr=sb_scr, base=bi * D_HEADS, y_ref=yb_ref, bi=bi))
    _gla_solve(items)


def _gla_scan(qk, v, small, gk_up_f, gk_b_f, gk_up_b, gk_b_b, n_ctx):
    bsz, n, _ = qk.shape
    L = D_CHUNK
    nc, ncc = n // L, n_ctx // L
    bw = _bwd_chunk(ncc, nc)
    hk = D_HEADS * D_KEY
    bias = jnp.zeros((8, hk), F32).at[0].set(gk_b_f).at[1].set(gk_b_b)
    nb = GLA_NB if bsz % GLA_NB == 0 else 1
    fx = lambda w: pl.BlockSpec((nb, L, w), lambda b, s: (b, s, 0))
    bx = lambda w: pl.BlockSpec((nb, L, w), lambda b, s: (b, bw(s), 0))
    return pl.pallas_call(
        _gla_scan_kernel,
        grid=(bsz // nb, nc),
        in_specs=[fx(2 * hk), fx(D_WIDTH), fx(128), bx(2 * hk), bx(D_WIDTH), bx(128),
                  _const_spec((D_LORA, hk)), _const_spec((D_LORA, hk)), _const_spec((8, hk))],
        out_specs=[fx(D_WIDTH), bx(D_WIDTH)],
        out_shape=[jax.ShapeDtypeStruct((bsz, n, D_WIDTH), F32)] * 2,
        scratch_shapes=[pltpu.VMEM((nb * D_HEADS, D_VAL, D_KEY), F32)] * 2,
        compiler_params=_cp("parallel", "arbitrary"),
    )(qk, v, small, qk, v, small, gk_up_f, gk_up_b, bias)


def _gla_post_kernel(yf_ref, yb_ref, g_ref, ng_ref, o_ref):
    y = yf_ref[...] + yb_ref[...]
    gate = _silu(g_ref[...])
    for h in range(D_HEADS):
        sl = slice(h * D_VAL, (h + 1) * D_VAL)
        blk = y[:, sl]
        ms = jnp.mean(blk * blk, axis=-1, keepdims=True)
        o_ref[:, sl] = blk * lax.rsqrt(ms + EPS) * ng_ref[...] * gate[:, sl]


def _gla_post(yf, yb, g, norm_g):
    bsz, n, _ = yf.shape
    tok = pl.BlockSpec((None, TM, D_WIDTH), lambda b, i: (b, i, 0))
    return pl.pallas_call(
        _gla_post_kernel,
        grid=(bsz, n // TM),
        in_specs=[tok, tok, tok, _const_spec((1, D_VAL))],
        out_specs=tok,
        out_shape=jax.ShapeDtypeStruct((bsz, n, D_WIDTH), F32),
        compiler_params=_cp("parallel", "parallel"),
    )(yf, yb, g, norm_g.reshape(1, D_VAL))


def kernel(x, c, ctx, c_ctx, l0_mod_w, l0_mod_b, l0_norm1, l0_norm2, l0_w_in, l0_w_out, l0_mlp_w1, l0_mlp_w2, l0_lam_q1, l0_lam_k1, l0_lam_q2, l0_lam_k2, l0_subln, l0_mu, l0_w0_f, l0_w2_f, l0_w0_b, l0_w2_b, l0_a0, l0_a2, l0_g2, l0_k_k, l0_k_a, l0_r_k, l0_lnx_w, l0_lnx_b, l1_mod_w, l1_mod_b, l1_norm1, l1_norm2, l1_w_in, l1_w_out, l1_mlp_w1, l1_mlp_w2, l1_conv_w, l1_conv_b, l1_dt_bias_f, l1_a_log_f, l1_dt_bias_b, l1_a_log_b, l1_d_skip, l1_ssm_norm, l1_gk_up_f, l1_gk_b_f, l1_gk_up_b, l1_gk_b_b, l1_gla_norm, final_norm):
    bsz, n_lat, d = x.shape
    n_ctx = ctx.shape[1]
    assert d == D and n_ctx % TM == 0 and n_lat % TM == 0 and n_lat % GRID_W == 0
    nct = n_ctx // TM
    xc = jnp.concatenate([ctx, x], axis=1)

    mod0 = _modulation(c, c_ctx, l0_mod_w, l0_mod_b)
    qkv, pb = _inproj0(xc, mod0, l0_norm1, l0_w_in, _rope_tables(n_ctx, n_lat), nct)
    lamv = jnp.stack([l0_lam_q1, l0_lam_k1, l0_lam_q2, l0_lam_k2], axis=0)
    oa = _attention(qkv, lamv, l0_subln, 0, n_ctx)
    ob = _rwkv_mix(pb, n_ctx, l0_mu, l0_w0_f, l0_w2_f, l0_w0_b, l0_w2_b, l0_a0, l0_a2, l0_g2,
                   l0_k_k, l0_k_a, l0_r_k, l0_lnx_w, l0_lnx_b)
    xc = _post(xc, oa, ob, mod0, l0_w_out, l0_norm2, l0_mlp_w1, l0_mlp_w2, None, nct, False)

    mod1 = _modulation(c, c_ctx, l1_mod_w, l1_mod_b)
    z, xbc, qk, v, gg, small = _inproj1(xc, mod1, l1_norm1, l1_w_in, nct)
    xact = _conv(xbc, l1_conv_w, l1_conv_b, nct)
    yf_c, yb_c = _ssd_scan(xact, small, jnp.swapaxes(small, 1, 2), l1_dt_bias_f, l1_a_log_f,
                           l1_dt_bias_b, l1_a_log_b, n_ctx)
    oc = _ssd_post(yf_c, yb_c, xact, z, l1_d_skip, l1_ssm_norm)
    yf_d, yb_d = _gla_scan(qk, v, small, l1_gk_up_f, l1_gk_b_f, l1_gk_up_b, l1_gk_b_b, n_ctx)
    od = _gla_post(yf_d, yb_d, gg, l1_gla_norm)
    return _post(xc, oc, od, mod1, l1_w_out, l1_norm2, l1_mlp_w1, l1_mlp_w2, final_norm, nct, True)
```

```python
import functools
import math

import jax
import jax.numpy as jnp
from jax import lax
from jax.experimental import pallas as pl
from jax.experimental.pallas import tpu as pltpu

F32 = jnp.float32
BF16 = jnp.bfloat16
HI = lax.Precision.HIGHEST

D = 1024
EPS = 1e-6
GRID_W = 64
ROPE_THETA = 10000.0
MLP_HIDDEN = 4 * D

A_HEADS, A_QK, A_V = 4, 64, 128
A_PROJ = 1536
A_SUBLN_EPS = 1e-5
B_HEADS, B_HEAD = 8, 64
B_WIDTH = 512
B_PROJ = 1792
B_GN_EPS = 64e-5
C_HEADS, C_HEAD, C_GROUPS, C_REP, C_STATE = 8, 64, 2, 4, 128
C_WIDTH = 512
C_CONV = 5
C_CHUNK = 128
C_CONV_DIM = 1024
D_HEADS, D_KEY, D_VAL = 4, 64, 128
D_WIDTH = 512
D_LORA = 16
D_TAU = 16.0
D_CHUNK = 64
ODD_PAD = 3200

TM = 256
TQ = 256
ATTN_SUB = 128
B_CHUNK = 64
B_GROUP = 4
ATTN_KC = 512
GLA_NB = 4
HALO = 8
VMEM_LIMIT = 56 * 1024 * 1024


def _cp(*sem):
    return pltpu.CompilerParams(dimension_semantics=sem, vmem_limit_bytes=VMEM_LIMIT)


def _const_spec(shape):
    nd = len(shape)
    return pl.BlockSpec(shape, lambda *_: (0,) * nd, pipeline_mode=pl.Buffered(1))


def _softplus(x):
    return jnp.maximum(x, 0.0) + jnp.log(1.0 + jnp.exp(-jnp.abs(x)))


def _sigmoid(x):
    return 1.0 / (1.0 + jnp.exp(-x))


def _silu(x):
    return x * _sigmoid(x)


def _ada(x, g, sh, sc):
    ms = jnp.mean(x * x, axis=-1, keepdims=True)
    return (x * lax.rsqrt(ms + EPS) * g) * (1.0 + sc) + sh


def _segsum(x, e, pieces=3):
    hi = x.astype(BF16)
    r1 = x - hi.astype(F32)
    mid = r1.astype(BF16)
    d = lambda a: jnp.dot(a, e, preferred_element_type=F32)
    if pieces == 2:
        return d(hi) + d(mid)
    lo = (r1 - mid.astype(F32)).astype(BF16)
    return d(hi) + d(mid) + d(lo)


def _block_ones(n, seg):
    i = jnp.arange(n) // seg
    return (i[:, None] == i[None, :]).astype(BF16)


def _tile_is_lat(nct):
    return lambda b, i: (b, (i >= nct).astype(jnp.int32), 0, 0)


def _mod_kernel(c_ref, w_ref, b_ref, o_ref):
    s = _silu(c_ref[...])
    o_ref[...] = jnp.dot(s, w_ref[...], precision=HI, preferred_element_type=F32) + b_ref[...]


def _modulation(c, c_ctx, mod_w, mod_b):
    bsz = c.shape[0]
    rows = 8 * ((bsz + 1 + 7) // 8)
    cs = jnp.zeros((rows, D), F32).at[0].set(c_ctx).at[1:bsz + 1].set(c)
    n6 = mod_w.shape[1]
    tn = 1024
    m = pl.pallas_call(
        _mod_kernel,
        grid=(n6 // tn,),
        in_specs=[pl.BlockSpec((rows, D), lambda j: (0, 0)),
                  pl.BlockSpec((D, tn), lambda j: (0, j)),
                  pl.BlockSpec((1, tn), lambda j: (0, j))],
        out_specs=pl.BlockSpec((rows, tn), lambda j: (0, j)),
        out_shape=jax.ShapeDtypeStruct((rows, n6), F32),
        compiler_params=_cp("arbitrary"),
    )(cs, mod_w, mod_b.reshape(1, n6))
    m_c = jnp.broadcast_to(m[0:1], (bsz, n6))
    return jnp.stack([m_c, m[1:bsz + 1]], axis=1)[:, :, None, :]


def _rope_tables(n_ctx, n_lat):
    t = jnp.arange(n_lat)
    row = (t // GRID_W).astype(F32)
    col = (t % GRID_W).astype(F32)
    nf = A_QK // 4
    inv = ROPE_THETA ** (-jnp.arange(nf, dtype=F32) / nf)
    ar = row[:, None] * inv[None, :]
    ac = col[:, None] * inv[None, :]
    z = jnp.zeros_like(ar)
    c64 = jnp.concatenate([jnp.cos(ar), jnp.cos(ar), jnp.cos(ac), jnp.cos(ac)], axis=-1)
    s1 = jnp.concatenate([-jnp.sin(ar), z, -jnp.sin(ac), z], axis=-1)
    s2 = jnp.concatenate([z, jnp.sin(ar), z, jnp.sin(ac)], axis=-1)
    pad = lambda a, v: jnp.concatenate([jnp.full((n_ctx, 128), v, F32), jnp.tile(a, (1, 2))], axis=0)
    return pad(c64, 1.0), pad(s1, 0.0), pad(s2, 0.0)


def _inproj0_kernel(x_ref, mod_ref, g_ref, wa_ref, wb_ref, c_ref, s1_ref, s2_ref, qkv_ref, pb_ref):
    mod = mod_ref[...]
    h = _ada(x_ref[...], g_ref[...], mod[:, 0:D], mod[:, D:2 * D]).astype(BF16)
    pa = jnp.dot(h, wa_ref[...], preferred_element_type=F32)
    c, s1, s2 = c_ref[...], s1_ref[...], s2_ref[...]
    scale = A_QK ** -0.5 * math.log2(math.e)
    for j in range(8):
        blk = pa[:, j * 128:(j + 1) * 128]
        rot = blk * c + pltpu.roll(blk, 112, 1) * s1 + pltpu.roll(blk, 16, 1) * s2
        if j < 4:
            rot = rot * scale
        qkv_ref[:, j * 128:(j + 1) * 128] = rot.astype(BF16)
    qkv_ref[:, 1024:1536] = pa[:, 1024:1536].astype(BF16)
    pb_ref[...] = jnp.dot(h, wb_ref[...], preferred_element_type=F32)


def _inproj0(xc, mod, norm1, w_in, tabs, nct):
    bsz, n, _ = xc.shape
    wa = w_in[:, :A_PROJ].astype(BF16)
    wb = w_in[:, A_PROJ:].astype(BF16)
    tok = lambda w: pl.BlockSpec((None, TM, w), lambda b, i: (b, i, 0))
    tab = pl.BlockSpec((TM, 128), lambda b, i: (i, 0))
    return pl.pallas_call(
        _inproj0_kernel,
        grid=(bsz, n // TM),
        in_specs=[tok(D), pl.BlockSpec((None, None, 1, 6 * D), _tile_is_lat(nct)),
                  _const_spec((1, D)), _const_spec((D, A_PROJ)), _const_spec((D, B_PROJ)),
                  tab, tab, tab],
        out_specs=[tok(A_PROJ), tok(B_PROJ)],
        out_shape=[jax.ShapeDtypeStruct((bsz, n, A_PROJ), BF16),
                   jax.ShapeDtypeStruct((bsz, n, B_PROJ), F32)],
        compiler_params=_cp("parallel", "parallel"),
    )(xc, mod, norm1.reshape(1, D), wa, wb, *tabs)


def _attn_kernel(lam_ref, q_ref, k_ref, v_ref, subln_ref, o_ref, vt_scr, *, lam_init, n_ctx, nq_ctx):
    qi = pl.program_id(2)
    n_all = k_ref.shape[0]

    @pl.when(qi == 0)
    def _():
        step = ATTN_KC
        for c0 in range(0, n_all, step):
            c1 = min(c0 + step, n_all)
            vt_scr[:, c0:c1] = v_ref[c0:c1, :].astype(F32).T.astype(BF16)

    lv = lam_ref[...]
    lam = (jnp.exp(jnp.sum(lv[0:1] * lv[1:2], axis=-1, keepdims=True))
           - jnp.exp(jnp.sum(lv[2:3] * lv[3:4], axis=-1, keepdims=True)) + lam_init)
    ts = ATTN_SUB
    nsub = q_ref.shape[0] // ts

    def tree(vals, op):
        while len(vals) > 1:
            vals = [op(vals[i], vals[i + 1]) if i + 1 < len(vals) else vals[i] for i in range(0, len(vals), 2)]
        return vals[0]

    def attend(chunks):
        subs = []
        for t in range(nsub + 1):
            cur = None
            if t < nsub:
                q = q_ref[t * ts:(t + 1) * ts, :]
                lane = lax.broadcasted_iota(jnp.int32, q.shape, 1)
                zero = jnp.zeros_like(q)
                q2 = jnp.concatenate([jnp.where(lane < A_QK, q, zero), jnp.where(lane >= A_QK, q, zero)], axis=0)
                cur = dict(q2=q2, ss=[], ms=[], ls=[], accs=[])
                subs.append(cur)
            prev = subs[t - 1] if t >= 1 else None
            for ci, (c0, cn) in enumerate(chunks):
                if cur is not None:
                    s = _nt_dot(k_ref[c0:c0 + cn, :], cur["q2"])
                    cur["ss"].append(s)
                    cur["ms"].append(jnp.max(s, axis=0, keepdims=True))
                if prev is not None:
                    p = jnp.exp2(prev["ss"][ci] - prev["m"])
                    prev["ls"].append(jnp.sum(p, axis=0, keepdims=True))
                    prev["accs"].append(jnp.dot(vt_scr[:, c0:c0 + cn], p.astype(BF16),
                                                preferred_element_type=F32))
            if cur is not None:
                cur["m"] = tree(cur["ms"], jnp.maximum)
            if prev is not None:
                on = tree(prev["accs"], jnp.add) / tree(prev["ls"], jnp.add)
                o = (on[:, :ts] - lam * on[:, ts:]).T
                ms = jnp.mean(o * o, axis=-1, keepdims=True)
                o_ref[(t - 1) * ts:t * ts, :] = ((o * lax.rsqrt(ms + A_SUBLN_EPS) * subln_ref[...])
                                                 * (1.0 - lam_init))

    kc = min(ATTN_KC, n_all - n_ctx)
    ctx_chunks = [(0, n_ctx)]
    lat_chunks = [(n_ctx + j * kc, kc) for j in range((n_all - n_ctx) // kc)]

    @pl.when(qi < nq_ctx)
    def _():
        attend(ctx_chunks)

    @pl.when(qi >= nq_ctx)
    def _():
        attend(ctx_chunks + lat_chunks)


def _attention(qkv, lamv, subln, layer_idx, n_ctx):
    bsz, n, _ = qkv.shape
    lam_init = 0.8 - 0.6 * math.exp(-0.3 * layer_idx)
    kern = functools.partial(_attn_kernel, lam_init=lam_init, n_ctx=n_ctx, nq_ctx=n_ctx // TQ)
    return pl.pallas_call(
        kern,
        grid=(bsz, A_HEADS, n // TQ),
        in_specs=[_const_spec((4, A_QK)),
                  pl.BlockSpec((None, TQ, 128), lambda b, h, i: (b, i, h)),
                  pl.BlockSpec((None, n, 128), lambda b, h, i: (b, 0, A_HEADS + h)),
                  pl.BlockSpec((None, n, 128), lambda b, h, i: (b, 0, 2 * A_HEADS + h)),
                  _const_spec((1, A_V))],
        out_specs=pl.BlockSpec((None, TQ, 128), lambda b, h, i: (b, i, h)),
        out_shape=jax.ShapeDtypeStruct((bsz, n, A_HEADS * A_V), F32),
        scratch_shapes=[pltpu.VMEM((A_V, n), BF16)],
        compiler_params=_cp("parallel", "parallel", "arbitrary"),
    )(lamv, qkv, qkv, qkv, subln.reshape(1, A_V))


def _halo_flags(i, nct, nt):
    prev_ok = jnp.logical_and(i != 0, i != nct)
    next_ok = jnp.logical_and(i != nct - 1, i != nt - 1)
    return prev_ok, next_ok


def _rwkv_prep_kernel(p_ref, pp_ref, pn_ref, mu_ref, vec_ref, w2f_ref, w2b_ref, a2_ref, g2_ref, e_ref,
                      r_ref, wf_ref, wb_ref, k_ref, v_ref, kk_ref, kb_ref, bonus_ref, g_ref, *, nct, nt):
    i = pl.program_id(1)
    f = p_ref[...]
    tm = f.shape[0]
    prev_ok, next_ok = _halo_flags(i, nct, nt)
    hp = jnp.where(prev_ok, pp_ref[HALO - 1:HALO, :], 0.0)
    hn = jnp.where(next_ok, pn_ref[0:1, :], 0.0)
    row = lax.broadcasted_iota(jnp.int32, f.shape, 0)
    prev = jnp.where(row == 0, hp, pltpu.roll(f, 1, 0))
    nxt = jnp.where(row == tm - 1, hn, pltpu.roll(f, tm - 1, 0))
    fm = f + (0.5 * (prev + nxt) - f) * mu_ref[...]
    r = fm[:, 0:512]
    k = fm[:, 512:1024]
    v = fm[:, 1024:1536]
    wd = fm[:, 1536:1600]
    ad = fm[:, 1600:1664]
    gd = fm[:, 1664:1792]
    w0f, w0b, a0 = vec_ref[0:1, :], vec_ref[1:2, :], vec_ref[2:3, :]
    k_k, k_a, r_k = vec_ref[3:4, :], vec_ref[4:5, :], vec_ref[5:6, :]
    e = e_ref[...]
    mm = lambda a, w_ref: jnp.dot(a.astype(BF16), w_ref[...], preferred_element_type=F32)
    a = _sigmoid(a0 + mm(ad, a2_ref))
    tw = jnp.tanh(wd)
    decay = lambda w0, w2_ref: -jnp.exp(-_softplus(-(w0 + mm(tw, w2_ref))) - 0.5)
    kkr = k * k_k
    kk = kkr / jnp.maximum(jnp.sqrt(_segsum(kkr * kkr, e)), 1e-12)
    k2 = k * (1.0 + (a - 1.0) * k_a)
    r_ref[...] = r
    wf_ref[...] = decay(w0f, w2f_ref)
    wb_ref[...] = decay(w0b, w2b_ref)
    k_ref[...] = k2
    v_ref[...] = v
    kk_ref[...] = kk
    kb_ref[...] = kk * a
    bonus_ref[...] = _segsum(r * k2 * r_k, e) * v
    g_ref[...] = mm(_sigmoid(gd), g2_ref)


def _rwkv_prep(pb, mu, vecs, w2_f, w2_b, a2, g2, nct):
    bsz, n, _ = pb.shape
    nt = n // TM
    hb = TM // HALO
    tok = lambda w: pl.BlockSpec((None, TM, w), lambda b, i: (b, i, 0))
    kern = functools.partial(_rwkv_prep_kernel, nct=nct, nt=nt)
    outs = pl.pallas_call(
        kern,
        grid=(bsz, nt),
        in_specs=[tok(B_PROJ),
                  pl.BlockSpec((None, HALO, B_PROJ), lambda b, i: (b, jnp.maximum(i * hb - 1, 0), 0)),
                  pl.BlockSpec((None, HALO, B_PROJ), lambda b, i: (b, jnp.minimum((i + 1) * hb, nt * hb - 1), 0)),
                  _const_spec((1, B_PROJ)), _const_spec((8, B_WIDTH)),
                  _const_spec((64, B_WIDTH)), _const_spec((64, B_WIDTH)), _const_spec((64, B_WIDTH)),
                  _const_spec((128, B_WIDTH)), _const_spec((B_WIDTH, B_WIDTH))],
        out_specs=[tok(B_WIDTH)] * 9,
        out_shape=[jax.ShapeDtypeStruct((bsz, n, B_WIDTH), F32)] * 9,
        compiler_params=_cp("parallel", "parallel"),
    )(pb, pb, pb, mu.reshape(1, B_PROJ), vecs, w2_f.astype(BF16), w2_b.astype(BF16), a2.astype(BF16),
      g2.astype(BF16), _block_ones(B_WIDTH, B_HEAD))
    return outs


def _cumsum3(tri_b, x):
    hi = x.astype(BF16)
    r1 = x - hi.astype(F32)
    mid = r1.astype(BF16)
    lo = (r1 - mid.astype(F32)).astype(BF16)
    d = lambda a: jnp.dot(tri_b, a, preferred_element_type=F32)
    return d(hi) + d(mid) + d(lo)


def _head_spread(heads, width):
    row = lax.broadcasted_iota(jnp.int32, (heads, heads * width), 0)
    col = lax.broadcasted_iota(jnp.int32, (heads, heads * width), 1)
    return (row == col // width).astype(BF16)


def _bdot(a, b):
    return jnp.dot(a.astype(BF16), b.astype(BF16), preferred_element_type=F32)


def _rwkv_dir(r, lw, k, v, kk, kb, s_scr, y_ref, causal):
    length = r.shape[0]
    assert length == B_HEAD
    gw = B_GROUP * B_HEAD
    ii = lax.broadcasted_iota(jnp.int32, (length, gw), 0)
    jj = lax.broadcasted_iota(jnp.int32, (length, gw), 1) % length
    tri = (ii >= jj) if causal else (ii <= jj)
    strict = jnp.logical_and(tri, ii != jj)
    lvl = []
    m = 1
    while m < length:
        same2 = (ii // (2 * m)) == (jj // (2 * m))
        diff1 = (ii // m) != (jj // m)
        lvl.append(jnp.logical_and(jnp.logical_and(same2, diff1), strict))
        m *= 2
    tri_sq, _ = _scan_masks(length, causal)
    cum = _cumsum3(tri_sq.astype(BF16), lw)
    last = length - 1 if causal else 0
    cl = cum[last:last + 1, :]
    gi = jnp.exp(-cum)
    gl = jnp.exp(cl - cum)
    kg = kk * jnp.exp(cum - lw)
    rg = r * jnp.exp(cum)
    ki, bi = k * gi, kb * gi
    kd, bd = k * gl, kb * gl
    el = jnp.exp(cl)
    probs = []
    for g in range(r.shape[1] // gw):
        sl = slice(g * gw, (g + 1) * gw)
        probs.append(dict(
            sl=sl, tri=tri, strict=strict, lvl=lvl, s_scr=s_scr, y_ref=y_ref, g=g,
            lhs=jnp.concatenate([kg[:, sl], rg[:, sl]], axis=0).astype(BF16),
            bi=bi[:, sl], ki=ki[:, sl], v=v[:, sl],
            kbd=jnp.concatenate([kd[:, sl], -bd[:, sl]], axis=0).astype(BF16),
            el=el[:, sl]))
    return probs


def _rwkv_solve(probs):
    length = probs[0]["v"].shape[0]
    gw = probs[0]["v"].shape[1]
    hrow = lax.broadcasted_iota(jnp.int32, (gw, gw), 0) // B_HEAD
    hcol = lax.broadcasted_iota(jnp.int32, (gw, gw), 1) // B_HEAD
    same_head = hrow == hcol
    same_head_b = same_head.astype(BF16)

    def bd(x):
        xb = x.astype(BF16)
        return jnp.concatenate([xb] * B_GROUP, axis=0) * same_head_b

    mm = lambda a, w: jnp.dot(a.astype(BF16), w, preferred_element_type=F32)
    for p in probs:
        p["sc_b"] = _nt_dot(p["lhs"], bd(p["bi"]))
        p["sc_k"] = _nt_dot(p["lhs"], bd(p["ki"]))
    for p in probs:
        p["a_kb"] = jnp.where(p["strict"], p["sc_b"][:length], 0.0)
        p["a_kk"] = jnp.where(p["strict"], p["sc_k"][:length], 0.0).astype(BF16)
        p["a_rb"] = jnp.where(p["tri"], p["sc_b"][length:], 0.0).astype(BF16)
        p["a_rk"] = jnp.where(p["tri"], p["sc_k"][length:], 0.0).astype(BF16)
        p["nn"] = -jnp.where(p["lvl"][0], p["a_kb"], 0.0)
        p["s0"] = p["s_scr"][p["g"]]
        p["vbd"] = bd(p["v"])
    for p in probs:
        p["hs"] = _nt_dot(p["lhs"], bd(p["s0"]))
        p["akv"] = mm(p["a_kk"], p["vbd"])
    for lv in range(1, len(probs[0]["lvl"])):
        for p in probs:
            aoff = jnp.where(p["lvl"][lv], p["a_kb"], 0.0)
            p["x"] = aoff + mm(p["nn"], bd(aoff))
        for p in probs:
            p["nn"] = p["nn"] - (p["x"] + mm(p["x"], bd(p["nn"])))
    for p in probs:
        rhs = p["hs"][:length] + p["akv"]
        p["u"] = rhs + mm(p["nn"], bd(rhs))
    for p in probs:
        p["y_ref"][:, p["sl"]] = p["hs"][length:] + mm(p["a_rk"], p["vbd"]) - mm(p["a_rb"], bd(p["u"]))
    for p in probs:
        vu = jnp.concatenate([p["v"], p["u"]], axis=0)
        z = jnp.dot(vu.T.astype(BF16), p["kbd"], preferred_element_type=F32)
        z = jnp.where(same_head, z, 0.0)
        fold = z[0:B_HEAD]
        for j in range(1, B_GROUP):
            fold = fold + z[j * B_HEAD:(j + 1) * B_HEAD]
        p["s_scr"][p["g"]] = p["s0"] * p["el"] + fold


def _rwkv_chunk_kernel(*refs):
    fwd, bwd = refs[0:6], refs[6:12]
    yf_ref, yb_ref, sf_scr, sb_scr = refs[12:16]

    @pl.when(pl.program_id(1) == 0)
    def _():
        sf_scr[...] = jnp.zeros_like(sf_scr)
        sb_scr[...] = jnp.zeros_like(sb_scr)

    probs = _rwkv_dir(*[x[...] for x in fwd], sf_scr, yf_ref, True)
    probs += _rwkv_dir(*[x[...] for x in bwd], sb_scr, yb_ref, False)
    _rwkv_solve(probs)


def _rwkv_chunk_scan(r, lwf, lwb, k, v, kk, kb, n_ctx):
    bsz, n, width = r.shape
    L = B_CHUNK
    nc, ncc = n // L, n_ctx // L
    bw = _bwd_chunk(ncc, nc)
    fx = pl.BlockSpec((None, L, width), lambda b, s: (b, s, 0))
    bx = pl.BlockSpec((None, L, width), lambda b, s: (b, bw(s), 0))
    return pl.pallas_call(
        _rwkv_chunk_kernel,
        grid=(bsz, nc),
        in_specs=[fx] * 6 + [bx] * 6,
        out_specs=[fx, bx],
        out_shape=[jax.ShapeDtypeStruct((bsz, n, width), F32)] * 2,
        scratch_shapes=[pltpu.VMEM((B_HEADS // B_GROUP, B_HEAD, B_GROUP * B_HEAD), F32)] * 2,
        compiler_params=_cp("parallel", "arbitrary"),
    )(r, lwf, k, v, kk, kb, r, lwb, k, v, kk, kb)


def _rwkv_mix(pb, n_ctx, mu, w0_f, w2_f, w0_b, w2_b, a0, a2, g2, k_k, k_a, r_k):
    zero = jnp.zeros_like(w0_f)
    vecs = jnp.stack([w0_f, w0_b, a0, k_k, k_a, r_k, zero, zero], axis=0)
    r, lwf, lwb, k2, v, kk, kb, bonus, g = _rwkv_prep(pb, mu, vecs, w2_f, w2_b, a2, g2, n_ctx // TM)
    yf, yb = _rwkv_chunk_scan(r, lwf, lwb, k2, v, kk, kb, n_ctx)
    return yf, yb, bonus, g


def _mix_even(tok, cst):
    oa, yf, yb, bonus, g = tok
    lnw, lnb, e = cst
    y = yf + yb
    d = y - _segsum(y, e, 2) * (1.0 / B_HEAD)
    var = _segsum(d * d, e, 2) * (1.0 / B_HEAD)
    return oa,(d * lax.rsqrt(var + B_GN_EPS) * lnw + lnb + bonus) * g


def _mix_odd(tok, cst):
    yfc, ybc, xs, z, yfd, ybd, gg = tok
    dsk, ngc, ngd = cst
    y = (yfc + ybc + xs * dsk) * _silu(z)
    gw = C_WIDTH // C_GROUPS
    oc = []
    for g in range(C_GROUPS):
        blk = y[:, g * gw:(g + 1) * gw]
        ms = jnp.mean(blk * blk, axis=-1, keepdims=True)
        oc.append(blk * lax.rsqrt(ms + EPS) * ngc[:, g * gw:(g + 1) * gw])
    y = yfd + ybd
    gate = _silu(gg)
    od = []
    for h in range(D_HEADS):
        sl = slice(h * D_VAL, (h + 1) * D_VAL)
        blk = y[:, sl]
        ms = jnp.mean(blk * blk, axis=-1, keepdims=True)
        od.append(blk * lax.rsqrt(ms + EPS) * ngd * gate[:, sl])
    return jnp.concatenate(oc, axis=-1), jnp.concatenate(od, axis=-1)


def _post_kernel(*refs, mix_fn, n_tok, n_cst, final):
    x_ref = refs[0]
    tok = [r[...] for r in refs[1:1 + n_tok]]
    cst = [r[...] for r in refs[1 + n_tok:1 + n_tok + n_cst]]
    mod_ref, woa_ref, wob_ref, n2_ref, w1_ref, w2_ref, fn_ref, out_ref = refs[1 + n_tok + n_cst:]
    mod = mod_ref[...]
    g1, sh2 = mod[:, 2 * D:3 * D], mod[:, 3 * D:4 * D]
    sc2, g2 = mod[:, 4 * D:5 * D], mod[:, 5 * D:6 * D]
    o1, o2 = mix_fn(tok, cst)
    mix = (jnp.dot(o1.astype(BF16), woa_ref[...], preferred_element_type=F32)
           + jnp.dot(o2.astype(BF16), wob_ref[...], preferred_element_type=F32))
    x1 = x_ref[...] + g1 * mix
    h = _ada(x1, n2_ref[...], sh2, sc2).astype(BF16)
    hid = jnp.maximum(jnp.dot(h, w1_ref[...], preferred_element_type=F32), 0.0)
    hid = (hid * hid).astype(BF16)
    x2 = x1 + g2 * jnp.dot(hid, w2_ref[...], preferred_element_type=F32)
    if final:
        ms = jnp.mean(x2 * x2, axis=-1, keepdims=True)
        x2 = x2 * lax.rsqrt(ms + EPS) * fn_ref[...]
    out_ref[...] = x2


def _post(xc, mix_fn, tok_in, cst_in, mod, w_out, norm2, mlp_w1, mlp_w2, final_norm, nct, lat_only):
    bsz, n, _ = xc.shape
    half = w_out.shape[0] // 2
    t0 = nct if lat_only else 0
    nt = n // TM - t0
    tok = lambda w: pl.BlockSpec((None, TM, w), lambda b, i: (b, i + t0, 0))
    final = final_norm is not None
    fn = (final_norm if final else jnp.ones((D,), F32)).reshape(1, D)
    kern = functools.partial(_post_kernel, mix_fn=mix_fn, n_tok=len(tok_in), n_cst=len(cst_in), final=final)
    return pl.pallas_call(
        kern,
        grid=(bsz, nt),
        in_specs=[tok(D)] + [tok(half)] * len(tok_in) + [_const_spec(a.shape) for a in cst_in]
                 + [pl.BlockSpec((None, None, 1, 6 * D), lambda b, i: (b, (i + t0 >= nct).astype(jnp.int32), 0, 0)),
                    _const_spec((half, D)), _const_spec((half, D)), _const_spec((1, D)),
                    _const_spec((D, MLP_HIDDEN)), _const_spec((MLP_HIDDEN, D)), _const_spec((1, D))],
        out_specs=pl.BlockSpec((None, TM, D), lambda b, i: (b, i, 0)),
        out_shape=jax.ShapeDtypeStruct((bsz, nt * TM, D), F32),
        compiler_params=_cp("parallel", "parallel"),
    )(xc, *tok_in, *cst_in, mod, w_out[:half].astype(BF16), w_out[half:].astype(BF16), norm2.reshape(1, D),
      mlp_w1.astype(BF16), mlp_w2.astype(BF16), fn)


def _inproj1_kernel(x_ref, mod_ref, g_ref, w_ref, z_ref, xbc_ref, qk_ref, v_ref, gg_ref, sm_ref):
    mod = mod_ref[...]
    h = _ada(x_ref[...], g_ref[...], mod[:, 0:D], mod[:, D:2 * D]).astype(BF16)
    p = jnp.dot(h, w_ref[...], preferred_element_type=F32)
    z_ref[...] = p[:, 0:512]
    xbc_ref[...] = p[:, 512:1536]
    qk_ref[...] = p[:, 1536:2048]
    v_ref[...] = p[:, 2048:2560]
    gg_ref[...] = p[:, 2560:3072]
    sm_ref[...] = p[:, 3072:3200]


def _inproj1(xc, mod, norm1, w_in, nct):
    bsz, n, _ = xc.shape
    c0 = C_WIDTH + C_CONV_DIM
    d0 = c0 + 2 * C_HEADS
    qk_w = 2 * D_HEADS * D_KEY
    w = jnp.concatenate([w_in[:, :c0], w_in[:, d0:d0 + qk_w + 2 * D_WIDTH], w_in[:, c0:d0],
                         w_in[:, d0 + qk_w + 2 * D_WIDTH:],
                         jnp.zeros((D, 128 - 2 * C_HEADS - D_LORA), F32)], axis=1).astype(BF16)
    tok = lambda wd: pl.BlockSpec((None, TM, wd), lambda b, i: (b, i, 0))
    widths = [512, 1024, 512, 512, 512, 128]
    return pl.pallas_call(
        _inproj1_kernel,
        grid=(bsz, n // TM),
        in_specs=[tok(D), pl.BlockSpec((None, None, 1, 6 * D), _tile_is_lat(nct)),
                  _const_spec((1, D)), _const_spec((D, ODD_PAD))],
        out_specs=[tok(wd) for wd in widths],
        out_shape=[jax.ShapeDtypeStruct((bsz, n, wd), F32) for wd in widths],
        compiler_params=_cp("parallel", "parallel"),
    )(xc, mod, norm1.reshape(1, D), w)


def _conv_kernel(x_ref, xp_ref, xn_ref, w_ref, b_ref, o_ref, *, nct, nt):
    i = pl.program_id(1)
    x = x_ref[...]
    tm = x.shape[0]
    prev_ok, next_ok = _halo_flags(i, nct, nt)
    hp = jnp.where(prev_ok, xp_ref[...], 0.0)
    hn = jnp.where(next_ok, xn_ref[...], 0.0)
    ext = jnp.concatenate([hp, x, hn], axis=0)
    ne = tm + 2 * HALO
    acc = b_ref[...] + w_ref[2:3, :] * x
    for j in (0, 1, 3, 4):
        shifted = pltpu.roll(ext, (C_CONV // 2 - j) % ne, 0)[HALO:HALO + tm]
        acc = acc + w_ref[j:j + 1, :] * shifted
    o_ref[...] = _silu(acc)


def _conv(xbc, conv_w, conv_b, nct):
    bsz, n, cdim = xbc.shape
    nt = n // TM
    hb = TM // HALO
    tok = pl.BlockSpec((None, TM, cdim), lambda b, i: (b, i, 0))
    w8 = jnp.zeros((8, cdim), F32).at[:C_CONV].set(conv_w)
    return pl.pallas_call(
        functools.partial(_conv_kernel, nct=nct, nt=nt),
        grid=(bsz, nt),
        in_specs=[tok,
                  pl.BlockSpec((None, HALO, cdim), lambda b, i: (b, jnp.maximum(i * hb - 1, 0), 0)),
                  pl.BlockSpec((None, HALO, cdim), lambda b, i: (b, jnp.minimum((i + 1) * hb, nt * hb - 1), 0)),
                  _const_spec((8, cdim)), _const_spec((1, cdim))],
        out_specs=tok,
        out_shape=jax.ShapeDtypeStruct((bsz, n, cdim), F32),
        compiler_params=_cp("parallel", "parallel"),
    )(xbc, xbc, xbc, w8, conv_b.reshape(1, cdim))


def _scan_masks(length, causal):
    ii = lax.broadcasted_iota(jnp.int32, (length, length), 0)
    jj = lax.broadcasted_iota(jnp.int32, (length, length), 1)
    tri = (ii >= jj) if causal else (ii <= jj)
    tri_t = (ii <= jj) if causal else (ii >= jj)
    return tri, tri_t


def _nt_dot(a, b):
    return lax.dot_general(a, b, (((1,), (1,)), ((), ())), preferred_element_type=F32)


def _ssd_dir(xa, sm, sm_t, col0, prm, prm_t, h_scr, y_ref, causal):
    length = xa.shape[0]
    dtb, alog = prm[0:1, col0:col0 + C_HEADS], prm[1:2, col0:col0 + C_HEADS]
    dtb_t, alog_t = prm_t[col0:col0 + C_HEADS, 0:1], prm_t[col0:col0 + C_HEADS, 1:2]
    dt = _softplus(sm[:, col0:col0 + C_HEADS] + dtb)
    la = dt * (-jnp.exp(alog))
    la_t = _softplus(sm_t[col0:col0 + C_HEADS, :] + dtb_t) * (-jnp.exp(alog_t))
    tri, tri_t = _scan_masks(length, causal)
    cum = jnp.dot(tri.astype(F32), la, precision=HI, preferred_element_type=F32)
    cum_t = jnp.dot(la_t, tri_t.astype(F32), precision=HI, preferred_element_type=F32)
    last = length - 1 if causal else 0
    ecl = jnp.exp(cum[last:last + 1, :])
    spread = lambda a, w: _segsum(a, _head_spread(C_HEADS, w))
    xdt_all = xa[:, :C_WIDTH] * spread(dt, C_HEAD)
    xw_all = xdt_all * spread(jnp.exp(cum[last:last + 1, :] - cum), C_HEAD)
    ech_all = spread(jnp.exp(cum), C_HEAD)
    cum_all = spread(cum, length)
    groups, probs = [], []
    for g in range(C_GROUPS):
        bg = xa[:, C_WIDTH + g * C_STATE:C_WIDTH + (g + 1) * C_STATE].astype(BF16)
        cg = xa[:, C_WIDTH + (C_GROUPS + g) * C_STATE:C_WIDTH + (C_GROUPS + g + 1) * C_STATE].astype(BF16)
        grp = dict(bg=bg, cg=cg)
        groups.append(grp)
        for rr in range(C_REP):
            h = g * C_REP + rr
            sl = slice(h * C_HEAD, (h + 1) * C_HEAD)
            probs.append(dict(
                grp=grp, h=h, h_scr=h_scr, tri=tri,
                diff=cum_all[:, h * length:(h + 1) * length] - cum_t[h:h + 1, :],
                xdt=xdt_all[:, sl].astype(BF16), xw=xw_all[:, sl], ech=ech_all[:, sl], ecl=ecl[:, h:h + 1]))
    return groups, probs, y_ref


def _ssd_solve(dirs):
    for groups, _, _ in dirs:
        for grp in groups:
            grp["cb"] = _nt_dot(grp["cg"], grp["bg"])
    probs = [p for _, ps, _ in dirs for p in ps]
    for p in probs:
        lmat = jnp.exp(jnp.where(p["tri"], p["diff"], -jnp.inf))
        p["m"] = (p["grp"]["cb"] * lmat).astype(BF16)
        p["hs"] = p["h_scr"][p["h"]]
    for p in probs:
        p["y"] = (jnp.dot(p["m"], p["xdt"], preferred_element_type=F32)
                  + _nt_dot(p["grp"]["cg"], p["hs"].astype(BF16)) * p["ech"])
    for p in probs:
        xw = p["xw"].T.astype(BF16)
        p["h_scr"][p["h"]] = p["hs"] * p["ecl"] + jnp.dot(xw, p["grp"]["bg"], preferred_element_type=F32)
    for _, ps, y_ref in dirs:
        y_ref[...] = jnp.concatenate([p["y"] for p in ps], axis=-1)


def _ssd_scan_kernel(xf_ref, smf_ref, smtf_ref, xb_ref, smb_ref, smtb_ref, prm_ref, prmt_ref,
                     yf_ref, yb_ref, hf_scr, hb_scr):
    @pl.when(pl.program_id(1) == 0)
    def _():
        hf_scr[...] = jnp.zeros_like(hf_scr)
        hb_scr[...] = jnp.zeros_like(hb_scr)

    prm, prm_t = prm_ref[...], prmt_ref[...]
    _ssd_solve([_ssd_dir(xf_ref[...], smf_ref[...], smtf_ref[...], 0, prm, prm_t, hf_scr, yf_ref, True),
                _ssd_dir(xb_ref[...], smb_ref[...], smtb_ref[...], C_HEADS, prm, prm_t, hb_scr, yb_ref, False)])


def _bwd_chunk(ncc, nc):
    return lambda s: jnp.where(s < ncc, ncc - 1 - s, nc - 1 - s + ncc)


def _ssd_scan(xact, small, small_t, dt_bias_f, a_log_f, dt_bias_b, a_log_b, n_ctx):
    bsz, n, cdim = xact.shape
    L = C_CHUNK
    nc, ncc = n // L, n_ctx // L
    bw = _bwd_chunk(ncc, nc)
    prm = jnp.zeros((8, 128), F32).at[0, :16].set(jnp.concatenate([dt_bias_f, dt_bias_b]))
    prm = prm.at[1, :16].set(jnp.concatenate([a_log_f, a_log_b]))
    fx = lambda w: pl.BlockSpec((None, L, w), lambda b, s: (b, s, 0))
    bx = lambda w: pl.BlockSpec((None, L, w), lambda b, s: (b, bw(s), 0))
    ft = pl.BlockSpec((None, 128, L), lambda b, s: (b, 0, s))
    bt = pl.BlockSpec((None, 128, L), lambda b, s: (b, 0, bw(s)))
    return pl.pallas_call(
        _ssd_scan_kernel,
        grid=(bsz, nc),
        in_specs=[fx(cdim), fx(128), ft, bx(cdim), bx(128), bt, _const_spec((8, 128)), _const_spec((128, 8))],
        out_specs=[fx(C_WIDTH), bx(C_WIDTH)],
        out_shape=[jax.ShapeDtypeStruct((bsz, n, C_WIDTH), F32)] * 2,
        scratch_shapes=[pltpu.VMEM((C_HEADS, C_HEAD, C_STATE), F32)] * 2,
        compiler_params=_cp("parallel", "arbitrary"),
    )(xact, small, small_t, xact, small, small_t, prm, prm.T)


def _gla_solve(items):
    hk = D_HEADS * D_KEY
    for it in items:
        gd = it["sm"][:, 2 * C_HEADS:2 * C_HEADS + D_LORA]
        it["zz"] = jnp.dot(gd, it["up"], precision=HI, preferred_element_type=F32) + it["bias"]
    for it in items:
        lg = -_softplus(-it["zz"]) * (1.0 / D_TAU)
        it["tri"], _ = _scan_masks(lg.shape[0], it["causal"])
        it["cum"] = _cumsum3(it["tri"].astype(BF16), lg)
    probs = []
    for it in items:
        cum = it["cum"]
        last = cum.shape[0] - 1 if it["causal"] else 0
        cl = cum[last:last + 1, :]
        q = it["qk"][:, :hk] * (D_KEY ** -0.5)
        k = it["qk"][:, hk:]
        qe = (q * jnp.exp(cum)).astype(BF16)
        ke = (k * jnp.exp(-cum)).astype(BF16)
        kd = (k * jnp.exp(cl - cum)).astype(BF16)
        el = jnp.exp(cl)
        for h in range(D_HEADS):
            sl = slice(h * D_KEY, (h + 1) * D_KEY)
            probs.append(dict(it=it, h=h, qe=qe[:, sl], ke=ke[:, sl], kd=kd[:, sl], el=el[:, sl],
                              vh=it["v"][:, h * D_VAL:(h + 1) * D_VAL]))
    for p in probs:
        p["att"] = jnp.where(p["it"]["tri"], _nt_dot(p["qe"], p["ke"]), 0.0).astype(BF16)
        p["st"] = p["it"]["st_scr"][p["it"]["base"] + p["h"]]
    for p in probs:
        y = jnp.dot(p["att"], p["vh"].astype(BF16), preferred_element_type=F32)
        p["it"]["y_ref"][p["it"]["bi"], :, p["h"] * D_VAL:(p["h"] + 1) * D_VAL] = (
            y + _nt_dot(p["qe"], p["st"].astype(BF16)))
    for p in probs:
        p["it"]["st_scr"][p["it"]["base"] + p["h"]] = (
            p["st"] * p["el"] + jnp.dot(p["vh"].T.astype(BF16), p["kd"], preferred_element_type=F32))


def _gla_scan_kernel(qkf_ref, vf_ref, smf_ref, qkb_ref, vb_ref, smb_ref, upf_ref, upb_ref, bias_ref,
                     yf_ref, yb_ref, sf_scr, sb_scr):
    @pl.when(pl.program_id(1) == 0)
    def _():
        sf_scr[...] = jnp.zeros_like(sf_scr)
        sb_scr[...] = jnp.zeros_like(sb_scr)

    items = []
    for bi in range(qkf_ref.shape[0]):
        items.append(dict(qk=qkf_ref[bi], v=vf_ref[bi], sm=smf_ref[bi], up=upf_ref[...], bias=bias_ref[0:1, :],
                          causal=True, st_scr=sf_scr, base=bi * D_HEADS, y_ref=yf_ref, bi=bi))
        items.append(dict(qk=qkb_ref[bi], v=vb_ref[bi], sm=smb_ref[bi], up=upb_ref[...], bias=bias_ref[1:2, :],
                          causal=False, st_scr=sb_scr, base=bi * D_HEADS, y_ref=yb_ref, bi=bi))
    _gla_solve(items)


def _gla_scan(qk, v, small, gk_up_f, gk_b_f, gk_up_b, gk_b_b, n_ctx):
    bsz, n, _ = qk.shape
    L = D_CHUNK
    nc, ncc = n // L, n_ctx // L
    bw = _bwd_chunk(ncc, nc)
    hk = D_HEADS * D_KEY
    bias = jnp.zeros((8, hk), F32).at[0].set(gk_b_f).at[1].set(gk_b_b)
    nb = GLA_NB if bsz % GLA_NB == 0 else 1
    fx = lambda w: pl.BlockSpec((nb, L, w), lambda b, s: (b, s, 0))
    bx = lambda w: pl.BlockSpec((nb, L, w), lambda b, s: (b, bw(s), 0))
    return pl.pallas_call(
        _gla_scan_kernel,
        grid=(bsz // nb, nc),
        in_specs=[fx(2 * hk), fx(D_WIDTH), fx(128), bx(2 * hk), bx(D_WIDTH), bx(128),
                  _const_spec((D_LORA, hk)), _const_spec((D_LORA, hk)), _const_spec((8, hk))],
        out_specs=[fx(D_WIDTH), bx(D_WIDTH)],
        out_shape=[jax.ShapeDtypeStruct((bsz, n, D_WIDTH), F32)] * 2,
        scratch_shapes=[pltpu.VMEM((nb * D_HEADS, D_VAL, D_KEY), F32)] * 2,
        compiler_params=_cp("parallel", "arbitrary"),
    )(qk, v, small, qk, v, small, gk_up_f, gk_up_b, bias)


def kernel(x, c, ctx, c_ctx, l0_mod_w, l0_mod_b, l0_norm1, l0_norm2, l0_w_in, l0_w_out, l0_mlp_w1, l0_mlp_w2, l0_lam_q1, l0_lam_k1, l0_lam_q2, l0_lam_k2, l0_subln, l0_mu, l0_w0_f, l0_w2_f, l0_w0_b, l0_w2_b, l0_a0, l0_a2, l0_g2, l0_k_k, l0_k_a, l0_r_k, l0_lnx_w, l0_lnx_b, l1_mod_w, l1_mod_b, l1_norm1, l1_norm2, l1_w_in, l1_w_out, l1_mlp_w1, l1_mlp_w2, l1_conv_w, l1_conv_b, l1_dt_bias_f, l1_a_log_f, l1_dt_bias_b, l1_a_log_b, l1_d_skip, l1_ssm_norm, l1_gk_up_f, l1_gk_b_f, l1_gk_up_b, l1_gk_b_b, l1_gla_norm, final_norm):
    bsz, n_lat, d = x.shape
    n_ctx = ctx.shape[1]
    assert d == D and n_ctx % TM == 0 and n_lat % TM == 0 and n_lat % GRID_W == 0
    nct = n_ctx // TM
    xc = jnp.concatenate([ctx, x], axis=1)

    mod0 = _modulation(c, c_ctx, l0_mod_w, l0_mod_b)
    qkv, pb = _inproj0(xc, mod0, l0_norm1, l0_w_in, _rope_tables(n_ctx, n_lat), nct)
    lamv = jnp.stack([l0_lam_q1, l0_lam_k1, l0_lam_q2, l0_lam_k2], axis=0)
    oa = _attention(qkv, lamv, l0_subln, 0, n_ctx)
    yf_b, yb_b, bonus, gate = _rwkv_mix(pb, n_ctx, l0_mu, l0_w0_f, l0_w2_f, l0_w0_b, l0_w2_b, l0_a0, l0_a2, l0_g2,
                                        l0_k_k, l0_k_a, l0_r_k)
    xc = _post(xc, _mix_even, [oa, yf_b, yb_b, bonus, gate],
               [l0_lnx_w.reshape(1, -1), l0_lnx_b.reshape(1, -1), _block_ones(B_WIDTH, B_HEAD)],
               mod0, l0_w_out, l0_norm2, l0_mlp_w1, l0_mlp_w2, None, nct, False)

    mod1 = _modulation(c, c_ctx, l1_mod_w, l1_mod_b)
    z, xbc, qk, v, gg, small = _inproj1(xc, mod1, l1_norm1, l1_w_in, nct)
    xact = _conv(xbc, l1_conv_w, l1_conv_b, nct)
    yf_c, yb_c = _ssd_scan(xact, small, jnp.swapaxes(small, 1, 2), l1_dt_bias_f, l1_a_log_f,
                           l1_dt_bias_b, l1_a_log_b, n_ctx)
    yf_d, yb_d = _gla_scan(qk, v, small, l1_gk_up_f, l1_gk_b_f, l1_gk_up_b, l1_gk_b_b, n_ctx)
    return _post(xc, _mix_odd, [yf_c, yb_c, xact, z, yf_d, yb_d, gg],
                 [jnp.repeat(l1_d_skip, C_HEAD).reshape(1, C_WIDTH), l1_ssm_norm.reshape(1, C_WIDTH),
                  l1_gla_norm.reshape(1, D_VAL)],
                 mod1, l1_w_out, l1_norm2, l1_mlp_w1, l1_mlp_w2, final_norm, nct, True)
```

```python
import functools
import math

import jax
import jax.numpy as jnp
from jax import lax
from jax.experimental import pallas as pl
from jax.experimental.pallas import tpu as pltpu

F32 = jnp.float32
BF16 = jnp.bfloat16
HI = lax.Precision.HIGHEST

D = 1024
EPS = 1e-6
GRID_W = 64
ROPE_THETA = 10000.0
MLP_HIDDEN = 4 * D

A_HEADS, A_QK, A_V = 4, 64, 128
A_PROJ = 1536
A_SUBLN_EPS = 1e-5
B_HEADS, B_HEAD = 8, 64
B_WIDTH = 512
B_PROJ = 1792
B_GN_EPS = 64e-5
C_HEADS, C_HEAD, C_GROUPS, C_REP, C_STATE = 8, 64, 2, 4, 128
C_WIDTH = 512
C_CONV = 5
C_CHUNK = 128
C_CONV_DIM = 1024
D_HEADS, D_KEY, D_VAL = 4, 64, 128
D_WIDTH = 512
D_LORA = 16
D_TAU = 16.0
D_CHUNK = 64
ODD_PAD = 3200

TM = 256
TQ = 128
B_CHUNK = 64
B_GROUP = 4
ATTN_KC = 512
SSD_NB = 2
GLA_NB = 4
HALO = 8
VMEM_LIMIT = 56 * 1024 * 1024


def _cp(*sem):
    return pltpu.CompilerParams(dimension_semantics=sem, vmem_limit_bytes=VMEM_LIMIT)


def _const_spec(shape):
    nd = len(shape)
    return pl.BlockSpec(shape, lambda *_: (0,) * nd, pipeline_mode=pl.Buffered(1))


def _softplus(x):
    return jnp.maximum(x, 0.0) + jnp.log(1.0 + jnp.exp(-jnp.abs(x)))


def _sigmoid(x):
    return 1.0 / (1.0 + jnp.exp(-x))


def _silu(x):
    return x * _sigmoid(x)


def _ada(x, g, sh, sc):
    ms = jnp.mean(x * x, axis=-1, keepdims=True)
    return (x * lax.rsqrt(ms + EPS) * g) * (1.0 + sc) + sh


def _segsum(x, e, pieces=3):
    hi = x.astype(BF16)
    r1 = x - hi.astype(F32)
    mid = r1.astype(BF16)
    d = lambda a: jnp.dot(a, e, preferred_element_type=F32)
    if pieces == 2:
        return d(hi) + d(mid)
    lo = (r1 - mid.astype(F32)).astype(BF16)
    return d(hi) + d(mid) + d(lo)


def _block_ones(n, seg):
    i = jnp.arange(n) // seg
    return (i[:, None] == i[None, :]).astype(BF16)


def _tile_is_lat(nct):
    return lambda b, i: (b, (i >= nct).astype(jnp.int32), 0, 0)


def _mod_kernel(c_ref, w_ref, b_ref, o_ref):
    s = _silu(c_ref[...])
    o_ref[...] = jnp.dot(s, w_ref[...], precision=HI, preferred_element_type=F32) + b_ref[...]


def _modulation(c, c_ctx, mod_w, mod_b):
    bsz = c.shape[0]
    rows = 8 * ((bsz + 1 + 7) // 8)
    cs = jnp.zeros((rows, D), F32).at[0].set(c_ctx).at[1:bsz + 1].set(c)
    n6 = mod_w.shape[1]
    tn = 1024
    m = pl.pallas_call(
        _mod_kernel,
        grid=(n6 // tn,),
        in_specs=[pl.BlockSpec((rows, D), lambda j: (0, 0)),
                  pl.BlockSpec((D, tn), lambda j: (0, j)),
                  pl.BlockSpec((1, tn), lambda j: (0, j))],
        out_specs=pl.BlockSpec((rows, tn), lambda j: (0, j)),
        out_shape=jax.ShapeDtypeStruct((rows, n6), F32),
        compiler_params=_cp("arbitrary"),
    )(cs, mod_w, mod_b.reshape(1, n6))
    m_c = jnp.broadcast_to(m[0:1], (bsz, n6))
    return jnp.stack([m_c, m[1:bsz + 1]], axis=1)[:, :, None, :]


def _rope_tables(n_ctx, n_lat):
    t = jnp.arange(n_lat)
    row = (t // GRID_W).astype(F32)
    col = (t % GRID_W).astype(F32)
    nf = A_QK // 4
    inv = ROPE_THETA ** (-jnp.arange(nf, dtype=F32) / nf)
    ar = row[:, None] * inv[None, :]
    ac = col[:, None] * inv[None, :]
    z = jnp.zeros_like(ar)
    c64 = jnp.concatenate([jnp.cos(ar), jnp.cos(ar), jnp.cos(ac), jnp.cos(ac)], axis=-1)
    s1 = jnp.concatenate([-jnp.sin(ar), z, -jnp.sin(ac), z], axis=-1)
    s2 = jnp.concatenate([z, jnp.sin(ar), z, jnp.sin(ac)], axis=-1)
    pad = lambda a, v: jnp.concatenate([jnp.full((n_ctx, 128), v, F32), jnp.tile(a, (1, 2))], axis=0)
    return pad(c64, 1.0), pad(s1, 0.0), pad(s2, 0.0)


def _inproj0_kernel(x_ref, mod_ref, g_ref, wa_ref, wb_ref, c_ref, s1_ref, s2_ref, qkv_ref, pb_ref):
    mod = mod_ref[...]
    h = _ada(x_ref[...], g_ref[...], mod[:, 0:D], mod[:, D:2 * D]).astype(BF16)
    pa = jnp.dot(h, wa_ref[...], preferred_element_type=F32)
    c, s1, s2 = c_ref[...], s1_ref[...], s2_ref[...]
    scale = A_QK ** -0.5 * math.log2(math.e)
    for j in range(8):
        blk = pa[:, j * 128:(j + 1) * 128]
        rot = blk * c + pltpu.roll(blk, 112, 1) * s1 + pltpu.roll(blk, 16, 1) * s2
        if j < 4:
            rot = rot * scale
        qkv_ref[:, j * 128:(j + 1) * 128] = rot.astype(BF16)
    qkv_ref[:, 1024:1536] = pa[:, 1024:1536].astype(BF16)
    pb_ref[...] = jnp.dot(h, wb_ref[...], preferred_element_type=F32)


def _inproj0(xc, mod, norm1, w_in, tabs, nct):
    bsz, n, _ = xc.shape
    wa = w_in[:, :A_PROJ].astype(BF16)
    wb = w_in[:, A_PROJ:].astype(BF16)
    tok = lambda w: pl.BlockSpec((None, TM, w), lambda b, i: (b, i, 0))
    tab = pl.BlockSpec((TM, 128), lambda b, i: (i, 0))
    return pl.pallas_call(
        _inproj0_kernel,
        grid=(bsz, n // TM),
        in_specs=[tok(D), pl.BlockSpec((None, None, 1, 6 * D), _tile_is_lat(nct)),
                  _const_spec((1, D)), _const_spec((D, A_PROJ)), _const_spec((D, B_PROJ)),
                  tab, tab, tab],
        out_specs=[tok(A_PROJ), tok(B_PROJ)],
        out_shape=[jax.ShapeDtypeStruct((bsz, n, A_PROJ), BF16),
                   jax.ShapeDtypeStruct((bsz, n, B_PROJ), F32)],
        compiler_params=_cp("parallel", "parallel"),
    )(xc, mod, norm1.reshape(1, D), wa, wb, *tabs)


def _attn_kernel(lam_ref, q_ref, k_ref, v_ref, subln_ref, o_ref, vt_scr, s0_scr, s1_scr, m0_scr, m1_scr, *,
                 lam_init, n_ctx, nq_ctx, nq):
    qi = pl.program_id(2)
    n_all = k_ref.shape[0]

    @pl.when(qi == 0)
    def _():
        step = ATTN_KC
        for c0 in range(0, n_all, step):
            c1 = min(c0 + step, n_all)
            vt_scr[:, c0:c1] = v_ref[c0:c1, :].astype(F32).T.astype(BF16)

    lv = lam_ref[...]
    lam = (jnp.exp(jnp.sum(lv[0:1] * lv[1:2], axis=-1, keepdims=True))
           - jnp.exp(jnp.sum(lv[2:3] * lv[3:4], axis=-1, keepdims=True)) + lam_init)
    ts = q_ref.shape[0]

    def tree(vals, op):
        while len(vals) > 1:
            vals = [op(vals[i], vals[i + 1]) if i + 1 < len(vals) else vals[i] for i in range(0, len(vals), 2)]
        return vals[0]

    def stage(chunks_cur, chunks_prev, parity):
        s_cur, s_prev = (s0_scr, s1_scr) if parity == 0 else (s1_scr, s0_scr)
        m_cur, m_prev_scr = (m0_scr, m1_scr) if parity == 0 else (m1_scr, m0_scr)
        if chunks_cur:
            q = q_ref[...]
            lane = lax.broadcasted_iota(jnp.int32, q.shape, 1)
            zero = jnp.zeros_like(q)
            q2 = jnp.concatenate([jnp.where(lane < A_QK, q, zero), jnp.where(lane >= A_QK, q, zero)], axis=0)
        if chunks_prev:
            m_prev = m_prev_scr[0:1, :]
        ms, ls, accs = [], [], []
        for ci in range(max(len(chunks_cur), len(chunks_prev))):
            if ci < len(chunks_cur):
                c0, cn = chunks_cur[ci]
                s = _nt_dot(k_ref[c0:c0 + cn, :], q2)
                s_cur[c0:c0 + cn, :] = s
                ms.append(jnp.max(s, axis=0, keepdims=True))
            if ci < len(chunks_prev):
                c0, cn = chunks_prev[ci]
                p = jnp.exp2(s_prev[c0:c0 + cn, :] - m_prev)
                ls.append(jnp.sum(p, axis=0, keepdims=True))
                accs.append(jnp.dot(vt_scr[:, c0:c0 + cn], p.astype(BF16), preferred_element_type=F32))
        if chunks_cur:
            m_cur[...] = jnp.broadcast_to(tree(ms, jnp.maximum), m_cur.shape)
        if chunks_prev:
            on = tree(accs, jnp.add) / tree(ls, jnp.add)
            o = (on[:, :ts] - lam * on[:, ts:]).T
            msq = jnp.mean(o * o, axis=-1, keepdims=True)
            o_ref[...] = (o * lax.rsqrt(msq + A_SUBLN_EPS) * subln_ref[...]) * (1.0 - lam_init)
        else:
            o_ref[...] = jnp.zeros_like(o_ref)

    kc = min(ATTN_KC, n_all - n_ctx)
    ctx_chunks = [(0, n_ctx)]
    all_chunks = ctx_chunks + [(n_ctx + j * kc, kc) for j in range((n_all - n_ctx) // kc)]
    for parity in (0, 1):
        on = lambda cond: pl.when(jnp.logical_and(cond, qi % 2 == parity))
        if parity == 0:
            on(qi == 0)(functools.partial(stage, ctx_chunks, [], parity))
        if nq_ctx > 1:
            on(jnp.logical_and(qi >= 1, qi < nq_ctx))(functools.partial(stage, ctx_chunks, ctx_chunks, parity))
        if nq_ctx % 2 == parity:
            on(qi == nq_ctx)(functools.partial(stage, all_chunks, ctx_chunks, parity))
        on(jnp.logical_and(qi > nq_ctx, qi < nq))(functools.partial(stage, all_chunks, all_chunks, parity))
        if nq % 2 == parity:
            on(qi == nq)(functools.partial(stage, [], all_chunks, parity))


def _attention(qkv, lamv, subln, layer_idx, n_ctx):
    bsz, n, _ = qkv.shape
    lam_init = 0.8 - 0.6 * math.exp(-0.3 * layer_idx)
    nq = n // TQ
    assert n_ctx % TQ == 0 and n_ctx < n
    kern = functools.partial(_attn_kernel, lam_init=lam_init, n_ctx=n_ctx, nq_ctx=n_ctx // TQ, nq=nq)
    return pl.pallas_call(
        kern,
        grid=(bsz, A_HEADS, nq + 1),
        in_specs=[_const_spec((4, A_QK)),
                  pl.BlockSpec((None, TQ, 128), lambda b, h, i: (b, jnp.minimum(i, nq - 1), h)),
                  pl.BlockSpec((None, n, 128), lambda b, h, i: (b, 0, A_HEADS + h)),
                  pl.BlockSpec((None, n, 128), lambda b, h, i: (b, 0, 2 * A_HEADS + h)),
                  _const_spec((1, A_V))],
        out_specs=pl.BlockSpec((None, TQ, 128), lambda b, h, i: (b, jnp.maximum(i - 1, 0), h)),
        out_shape=jax.ShapeDtypeStruct((bsz, n, A_HEADS * A_V), F32),
        scratch_shapes=[pltpu.VMEM((A_V, n), BF16), pltpu.VMEM((n, 2 * TQ), F32), pltpu.VMEM((n, 2 * TQ), F32),
                        pltpu.VMEM((8, 2 * TQ), F32), pltpu.VMEM((8, 2 * TQ), F32)],
        compiler_params=_cp("parallel", "parallel", "arbitrary"),
    )(lamv, qkv, qkv, qkv, subln.reshape(1, A_V))


def _halo_flags(i, nct, nt):
    prev_ok = jnp.logical_and(i != 0, i != nct)
    next_ok = jnp.logical_and(i != nct - 1, i != nt - 1)
    return prev_ok, next_ok


def _rwkv_prep_kernel(p_ref, pp_ref, pn_ref, mu_ref, vec_ref, w2f_ref, w2b_ref, a2_ref, g2_ref, e_ref,
                      r_ref, wf_ref, wb_ref, k_ref, v_ref, kk_ref, kb_ref, bonus_ref, g_ref, *, nct, nt):
    i = pl.program_id(1)
    f = p_ref[...]
    tm = f.shape[0]
    prev_ok, next_ok = _halo_flags(i, nct, nt)
    hp = jnp.where(prev_ok, pp_ref[HALO - 1:HALO, :], 0.0)
    hn = jnp.where(next_ok, pn_ref[0:1, :], 0.0)
    row = lax.broadcasted_iota(jnp.int32, f.shape, 0)
    prev = jnp.where(row == 0, hp, pltpu.roll(f, 1, 0))
    nxt = jnp.where(row == tm - 1, hn, pltpu.roll(f, tm - 1, 0))
    fm = f + (0.5 * (prev + nxt) - f) * mu_ref[...]
    r = fm[:, 0:512]
    k = fm[:, 512:1024]
    v = fm[:, 1024:1536]
    wd = fm[:, 1536:1600]
    ad = fm[:, 1600:1664]
    gd = fm[:, 1664:1792]
    w0f, w0b, a0 = vec_ref[0:1, :], vec_ref[1:2, :], vec_ref[2:3, :]
    k_k, k_a, r_k = vec_ref[3:4, :], vec_ref[4:5, :], vec_ref[5:6, :]
    e = e_ref[...]
    mm = lambda a, w_ref: jnp.dot(a.astype(BF16), w_ref[...], preferred_element_type=F32)
    a = _sigmoid(a0 + mm(ad, a2_ref))
    tw = jnp.tanh(wd)
    decay = lambda w0, w2_ref: -jnp.exp(-_softplus(-(w0 + mm(tw, w2_ref))) - 0.5)
    kkr = k * k_k
    kk = kkr / jnp.maximum(jnp.sqrt(_segsum(kkr * kkr, e)), 1e-12)
    k2 = k * (1.0 + (a - 1.0) * k_a)
    r_ref[...] = r
    wf_ref[...] = decay(w0f, w2f_ref)
    wb_ref[...] = decay(w0b, w2b_ref)
    k_ref[...] = k2
    v_ref[...] = v
    kk_ref[...] = kk
    kb_ref[...] = kk * a
    bonus_ref[...] = _segsum(r * k2 * r_k, e) * v
    g_ref[...] = mm(_sigmoid(gd), g2_ref)


def _rwkv_prep(pb, mu, vecs, w2_f, w2_b, a2, g2, nct):
    bsz, n, _ = pb.shape
    nt = n // TM
    hb = TM // HALO
    tok = lambda w: pl.BlockSpec((None, TM, w), lambda b, i: (b, i, 0))
    kern = functools.partial(_rwkv_prep_kernel, nct=nct, nt=nt)
    outs = pl.pallas_call(
        kern,
        grid=(bsz, nt),
        in_specs=[tok(B_PROJ),
                  pl.BlockSpec((None, HALO, B_PROJ), lambda b, i: (b, jnp.maximum(i * hb - 1, 0), 0)),
                  pl.BlockSpec((None, HALO, B_PROJ), lambda b, i: (b, jnp.minimum((i + 1) * hb, nt * hb - 1), 0)),
                  _const_spec((1, B_PROJ)), _const_spec((8, B_WIDTH)),
                  _const_spec((64, B_WIDTH)), _const_spec((64, B_WIDTH)), _const_spec((64, B_WIDTH)),
                  _const_spec((128, B_WIDTH)), _const_spec((B_WIDTH, B_WIDTH))],
        out_specs=[tok(B_WIDTH)] * 9,
        out_shape=[jax.ShapeDtypeStruct((bsz, n, B_WIDTH), F32)] * 9,
        compiler_params=_cp("parallel", "parallel"),
    )(pb, pb, pb, mu.reshape(1, B_PROJ), vecs, w2_f.astype(BF16), w2_b.astype(BF16), a2.astype(BF16),
      g2.astype(BF16), _block_ones(B_WIDTH, B_HEAD))
    return outs


def _cumsum3(tri_b, x):
    hi = x.astype(BF16)
    r1 = x - hi.astype(F32)
    mid = r1.astype(BF16)
    lo = (r1 - mid.astype(F32)).astype(BF16)
    d = lambda a: jnp.dot(tri_b, a, preferred_element_type=F32)
    return d(hi) + d(mid) + d(lo)


def _head_spread(heads, width):
    row = lax.broadcasted_iota(jnp.int32, (heads, heads * width), 0)
    col = lax.broadcasted_iota(jnp.int32, (heads, heads * width), 1)
    return (row == col // width).astype(BF16)


def _bdot(a, b):
    return jnp.dot(a.astype(BF16), b.astype(BF16), preferred_element_type=F32)


def _rwkv_dir(r, lw, k, v, kk, kb, s_scr, y_ref, causal):
    length = r.shape[0]
    assert length == B_HEAD
    gw = B_GROUP * B_HEAD
    ii = lax.broadcasted_iota(jnp.int32, (length, gw), 0)
    jj = lax.broadcasted_iota(jnp.int32, (length, gw), 1) % length
    tri = (ii >= jj) if causal else (ii <= jj)
    strict = jnp.logical_and(tri, ii != jj)
    lvl = []
    m = 1
    while m < length:
        same2 = (ii // (2 * m)) == (jj // (2 * m))
        diff1 = (ii // m) != (jj // m)
        lvl.append(jnp.logical_and(jnp.logical_and(same2, diff1), strict))
        m *= 2
    tri_sq, _ = _scan_masks(length, causal)
    cum = _cumsum3(tri_sq.astype(BF16), lw)
    last = length - 1 if causal else 0
    cl = cum[last:last + 1, :]
    gi = jnp.exp(-cum)
    gl = jnp.exp(cl - cum)
    kg = kk * jnp.exp(cum - lw)
    rg = r * jnp.exp(cum)
    ki, bi = k * gi, kb * gi
    kd, bd = k * gl, kb * gl
    el = jnp.exp(cl)
    probs = []
    for g in range(r.shape[1] // gw):
        sl = slice(g * gw, (g + 1) * gw)
        probs.append(dict(
            sl=sl, tri=tri, strict=strict, lvl=lvl, s_scr=s_scr, y_ref=y_ref, g=g,
            lhs=jnp.concatenate([kg[:, sl], rg[:, sl]], axis=0).astype(BF16),
            bi=bi[:, sl], ki=ki[:, sl], v=v[:, sl],
            kbd=jnp.concatenate([kd[:, sl], -bd[:, sl]], axis=0).astype(BF16),
            el=el[:, sl]))
    return probs


def _rwkv_solve(probs):
    length = probs[0]["v"].shape[0]
    gw = probs[0]["v"].shape[1]
    hrow = lax.broadcasted_iota(jnp.int32, (gw, gw), 0) // B_HEAD
    hcol = lax.broadcasted_iota(jnp.int32, (gw, gw), 1) // B_HEAD
    same_head = hrow == hcol
    same_head_b = same_head.astype(BF16)

    def bd(x):
        xb = x.astype(BF16)
        return jnp.concatenate([xb] * B_GROUP, axis=0) * same_head_b

    mm = lambda a, w: jnp.dot(a.astype(BF16), w, preferred_element_type=F32)
    for p in probs:
        p["sc_b"] = _nt_dot(p["lhs"], bd(p["bi"]))
        p["sc_k"] = _nt_dot(p["lhs"], bd(p["ki"]))
    for p in probs:
        p["a_kb"] = jnp.where(p["strict"], p["sc_b"][:length], 0.0)
        p["a_kk"] = jnp.where(p["strict"], p["sc_k"][:length], 0.0).astype(BF16)
        p["a_rb"] = jnp.where(p["tri"], p["sc_b"][length:], 0.0).astype(BF16)
        p["a_rk"] = jnp.where(p["tri"], p["sc_k"][length:], 0.0).astype(BF16)
        p["nn"] = -jnp.where(p["lvl"][0], p["a_kb"], 0.0)
        p["s0"] = p["s_scr"][p["g"]]
        p["vbd"] = bd(p["v"])
    for p in probs:
        p["hs"] = _nt_dot(p["lhs"], bd(p["s0"]))
        p["akv"] = mm(p["a_kk"], p["vbd"])
    for lv in range(1, len(probs[0]["lvl"])):
        for p in probs:
            aoff = jnp.where(p["lvl"][lv], p["a_kb"], 0.0)
            p["x"] = aoff + mm(p["nn"], bd(aoff))
        for p in probs:
            p["nn"] = p["nn"] - (p["x"] + mm(p["x"], bd(p["nn"])))
    for p in probs:
        rhs = p["hs"][:length] + p["akv"]
        p["u"] = rhs + mm(p["nn"], bd(rhs))
    for p in probs:
        p["y_ref"][:, p["sl"]] = p["hs"][length:] + mm(p["a_rk"], p["vbd"]) - mm(p["a_rb"], bd(p["u"]))
    for p in probs:
        vu = jnp.concatenate([p["v"], p["u"]], axis=0)
        z = jnp.dot(vu.T.astype(BF16), p["kbd"], preferred_element_type=F32)
        z = jnp.where(same_head, z, 0.0)
        fold = z[0:B_HEAD]
        for j in range(1, B_GROUP):
            fold = fold + z[j * B_HEAD:(j + 1) * B_HEAD]
        p["s_scr"][p["g"]] = p["s0"] * p["el"] + fold


def _rwkv_chunk_kernel(*refs):
    fwd, bwd = refs[0:6], refs[6:12]
    yf_ref, yb_ref, sf_scr, sb_scr = refs[12:16]

    @pl.when(pl.program_id(1) == 0)
    def _():
        sf_scr[...] = jnp.zeros_like(sf_scr)
        sb_scr[...] = jnp.zeros_like(sb_scr)

    probs = _rwkv_dir(*[x[...] for x in fwd], sf_scr, yf_ref, True)
    probs += _rwkv_dir(*[x[...] for x in bwd], sb_scr, yb_ref, False)
    _rwkv_solve(probs)


def _rwkv_chunk_scan(r, lwf, lwb, k, v, kk, kb, n_ctx):
    bsz, n, width = r.shape
    L = B_CHUNK
    nc, ncc = n // L, n_ctx // L
    bw = _bwd_chunk(ncc, nc)
    fx = pl.BlockSpec((None, L, width), lambda b, s: (b, s, 0))
    bx = pl.BlockSpec((None, L, width), lambda b, s: (b, bw(s), 0))
    return pl.pallas_call(
        _rwkv_chunk_kernel,
        grid=(bsz, nc),
        in_specs=[fx] * 6 + [bx] * 6,
        out_specs=[fx, bx],
        out_shape=[jax.ShapeDtypeStruct((bsz, n, width), F32)] * 2,
        scratch_shapes=[pltpu.VMEM((B_HEADS // B_GROUP, B_HEAD, B_GROUP * B_HEAD), F32)] * 2,
        compiler_params=_cp("parallel", "arbitrary"),
    )(r, lwf, k, v, kk, kb, r, lwb, k, v, kk, kb)


def _rwkv_mix(pb, n_ctx, mu, w0_f, w2_f, w0_b, w2_b, a0, a2, g2, k_k, k_a, r_k):
    zero = jnp.zeros_like(w0_f)
    vecs = jnp.stack([w0_f, w0_b, a0, k_k, k_a, r_k, zero, zero], axis=0)
    r, lwf, lwb, k2, v, kk, kb, bonus, g = _rwkv_prep(pb, mu, vecs, w2_f, w2_b, a2, g2, n_ctx // TM)
    yf, yb = _rwkv_chunk_scan(r, lwf, lwb, k2, v, kk, kb, n_ctx)
    return yf, yb, bonus, g


def _mix_even(tok, cst):
    oa, yf, yb, bonus, g = tok
    lnw, lnb, e = cst
    y = yf + yb
    d = y - _segsum(y, e, 2) * (1.0 / B_HEAD)
    var = _segsum(d * d, e, 2) * (1.0 / B_HEAD)
    return oa,(d * lax.rsqrt(var + B_GN_EPS) * lnw + lnb + bonus) * g


def _mix_odd(tok, cst):
    yfc, ybc, xs, z, yfd, ybd, gg = tok
    dsk, ngc, ngd = cst
    y = (yfc + ybc + xs * dsk) * _silu(z)
    gw = C_WIDTH // C_GROUPS
    oc = []
    for g in range(C_GROUPS):
        blk = y[:, g * gw:(g + 1) * gw]
        ms = jnp.mean(blk * blk, axis=-1, keepdims=True)
        oc.append(blk * lax.rsqrt(ms + EPS) * ngc[:, g * gw:(g + 1) * gw])
    y = yfd + ybd
    gate = _silu(gg)
    od = []
    for h in range(D_HEADS):
        sl = slice(h * D_VAL, (h + 1) * D_VAL)
        blk = y[:, sl]
        ms = jnp.mean(blk * blk, axis=-1, keepdims=True)
        od.append(blk * lax.rsqrt(ms + EPS) * ngd * gate[:, sl])
    return jnp.concatenate(oc, axis=-1), jnp.concatenate(od, axis=-1)


def _post_kernel(*refs, mix_fn, n_tok, n_cst, final):
    x_ref = refs[0]
    tok = [r[...] for r in refs[1:1 + n_tok]]
    cst = [r[...] for r in refs[1 + n_tok:1 + n_tok + n_cst]]
    mod_ref, woa_ref, wob_ref, n2_ref, w1_ref, w2_ref, fn_ref, out_ref = refs[1 + n_tok + n_cst:]
    mod = mod_ref[...]
    g1, sh2 = mod[:, 2 * D:3 * D], mod[:, 3 * D:4 * D]
    sc2, g2 = mod[:, 4 * D:5 * D], mod[:, 5 * D:6 * D]
    o1, o2 = mix_fn(tok, cst)
    mix = (jnp.dot(o1.astype(BF16), woa_ref[...], preferred_element_type=F32)
           + jnp.dot(o2.astype(BF16), wob_ref[...], preferred_element_type=F32))
    x1 = x_ref[...] + g1 * mix
    h = _ada(x1, n2_ref[...], sh2, sc2).astype(BF16)
    hid = jnp.maximum(jnp.dot(h, w1_ref[...], preferred_element_type=F32), 0.0)
    hid = (hid * hid).astype(BF16)
    x2 = x1 + g2 * jnp.dot(hid, w2_ref[...], preferred_element_type=F32)
    if final:
        ms = jnp.mean(x2 * x2, axis=-1, keepdims=True)
        x2 = x2 * lax.rsqrt(ms + EPS) * fn_ref[...]
    out_ref[...] = x2


def _post(xc, mix_fn, tok_in, cst_in, mod, w_out, norm2, mlp_w1, mlp_w2, final_norm, nct, lat_only):
    bsz, n, _ = xc.shape
    half = w_out.shape[0] // 2
    t0 = nct if lat_only else 0
    nt = n // TM - t0
    tok = lambda w: pl.BlockSpec((None, TM, w), lambda b, i: (b, i + t0, 0))
    final = final_norm is not None
    fn = (final_norm if final else jnp.ones((D,), F32)).reshape(1, D)
    kern = functools.partial(_post_kernel, mix_fn=mix_fn, n_tok=len(tok_in), n_cst=len(cst_in), final=final)
    return pl.pallas_call(
        kern,
        grid=(bsz, nt),
        in_specs=[tok(D)] + [tok(half)] * len(tok_in) + [_const_spec(a.shape) for a in cst_in]
                 + [pl.BlockSpec((None, None, 1, 6 * D), lambda b, i: (b, (i + t0 >= nct).astype(jnp.int32), 0, 0)),
                    _const_spec((half, D)), _const_spec((half, D)), _const_spec((1, D)),
                    _const_spec((D, MLP_HIDDEN)), _const_spec((MLP_HIDDEN, D)), _const_spec((1, D))],
        out_specs=pl.BlockSpec((None, TM, D), lambda b, i: (b, i, 0)),
        out_shape=jax.ShapeDtypeStruct((bsz, nt * TM, D), F32),
        compiler_params=_cp("parallel", "parallel"),
    )(xc, *tok_in, *cst_in, mod, w_out[:half].astype(BF16), w_out[half:].astype(BF16), norm2.reshape(1, D),
      mlp_w1.astype(BF16), mlp_w2.astype(BF16), fn)


def _inproj1_kernel(x_ref, mod_ref, g_ref, w_ref, z_ref, xbc_ref, qk_ref, v_ref, gg_ref, sm_ref):
    mod = mod_ref[...]
    h = _ada(x_ref[...], g_ref[...], mod[:, 0:D], mod[:, D:2 * D]).astype(BF16)
    p = jnp.dot(h, w_ref[...], preferred_element_type=F32)
    z_ref[...] = p[:, 0:512]
    xbc_ref[...] = p[:, 512:1536]
    qk_ref[...] = p[:, 1536:2048]
    v_ref[...] = p[:, 2048:2560]
    gg_ref[...] = p[:, 2560:3072]
    sm_ref[...] = p[:, 3072:3200]


def _inproj1(xc, mod, norm1, w_in, nct):
    bsz, n, _ = xc.shape
    c0 = C_WIDTH + C_CONV_DIM
    d0 = c0 + 2 * C_HEADS
    qk_w = 2 * D_HEADS * D_KEY
    w = jnp.concatenate([w_in[:, :c0], w_in[:, d0:d0 + qk_w + 2 * D_WIDTH], w_in[:, c0:d0],
                         w_in[:, d0 + qk_w + 2 * D_WIDTH:],
                         jnp.zeros((D, 128 - 2 * C_HEADS - D_LORA), F32)], axis=1).astype(BF16)
    tok = lambda wd: pl.BlockSpec((None, TM, wd), lambda b, i: (b, i, 0))
    widths = [512, 1024, 512, 512, 512, 128]
    return pl.pallas_call(
        _inproj1_kernel,
        grid=(bsz, n // TM),
        in_specs=[tok(D), pl.BlockSpec((None, None, 1, 6 * D), _tile_is_lat(nct)),
                  _const_spec((1, D)), _const_spec((D, ODD_PAD))],
        out_specs=[tok(wd) for wd in widths],
        out_shape=[jax.ShapeDtypeStruct((bsz, n, wd), F32) for wd in widths],
        compiler_params=_cp("parallel", "parallel"),
    )(xc, mod, norm1.reshape(1, D), w)


def _conv_kernel(x_ref, xp_ref, xn_ref, w_ref, b_ref, o_ref, *, nct, nt):
    i = pl.program_id(1)
    x = x_ref[...]
    tm = x.shape[0]
    prev_ok, next_ok = _halo_flags(i, nct, nt)
    hp = jnp.where(prev_ok, xp_ref[...], 0.0)
    hn = jnp.where(next_ok, xn_ref[...], 0.0)
    ext = jnp.concatenate([hp, x, hn], axis=0)
    ne = tm + 2 * HALO
    acc = b_ref[...] + w_ref[2:3, :] * x
    for j in (0, 1, 3, 4):
        shifted = pltpu.roll(ext, (C_CONV // 2 - j) % ne, 0)[HALO:HALO + tm]
        acc = acc + w_ref[j:j + 1, :] * shifted
    o_ref[...] = _silu(acc)


def _conv(xbc, conv_w, conv_b, nct):
    bsz, n, cdim = xbc.shape
    nt = n // TM
    hb = TM // HALO
    tok = pl.BlockSpec((None, TM, cdim), lambda b, i: (b, i, 0))
    w8 = jnp.zeros((8, cdim), F32).at[:C_CONV].set(conv_w)
    return pl.pallas_call(
        functools.partial(_conv_kernel, nct=nct, nt=nt),
        grid=(bsz, nt),
        in_specs=[tok,
                  pl.BlockSpec((None, HALO, cdim), lambda b, i: (b, jnp.maximum(i * hb - 1, 0), 0)),
                  pl.BlockSpec((None, HALO, cdim), lambda b, i: (b, jnp.minimum((i + 1) * hb, nt * hb - 1), 0)),
                  _const_spec((8, cdim)), _const_spec((1, cdim))],
        out_specs=tok,
        out_shape=jax.ShapeDtypeStruct((bsz, n, cdim), F32),
        compiler_params=_cp("parallel", "parallel"),
    )(xbc, xbc, xbc, w8, conv_b.reshape(1, cdim))


def _scan_masks(length, causal):
    ii = lax.broadcasted_iota(jnp.int32, (length, length), 0)
    jj = lax.broadcasted_iota(jnp.int32, (length, length), 1)
    tri = (ii >= jj) if causal else (ii <= jj)
    tri_t = (ii <= jj) if causal else (ii >= jj)
    return tri, tri_t


def _nt_dot(a, b):
    return lax.dot_general(a, b, (((1,), (1,)), ((), ())), preferred_element_type=F32)


def _ssd_dir(xa, sm, sm_t, col0, prm, prm_t, h_scr, y_ref, causal, bi):
    length = xa.shape[0]
    dtb, alog = prm[0:1, col0:col0 + C_HEADS], prm[1:2, col0:col0 + C_HEADS]
    dtb_t, alog_t = prm_t[col0:col0 + C_HEADS, 0:1], prm_t[col0:col0 + C_HEADS, 1:2]
    dt = _softplus(sm[:, col0:col0 + C_HEADS] + dtb)
    la = dt * (-jnp.exp(alog))
    la_t = _softplus(sm_t[col0:col0 + C_HEADS, :] + dtb_t) * (-jnp.exp(alog_t))
    tri, tri_t = _scan_masks(length, causal)
    cum = jnp.dot(tri.astype(F32), la, precision=HI, preferred_element_type=F32)
    cum_t = jnp.dot(la_t, tri_t.astype(F32), precision=HI, preferred_element_type=F32)
    last = length - 1 if causal else 0
    ecl = jnp.exp(cum[last:last + 1, :])
    spread = lambda a, w: _segsum(a, _head_spread(C_HEADS, w))
    xdt_all = xa[:, :C_WIDTH] * spread(dt, C_HEAD)
    xw_all = xdt_all * spread(jnp.exp(cum[last:last + 1, :] - cum), C_HEAD)
    ech_all = spread(jnp.exp(cum), C_HEAD)
    cum_all = spread(cum, length)
    groups, probs = [], []
    for g in range(C_GROUPS):
        bg = xa[:, C_WIDTH + g * C_STATE:C_WIDTH + (g + 1) * C_STATE].astype(BF16)
        cg = xa[:, C_WIDTH + (C_GROUPS + g) * C_STATE:C_WIDTH + (C_GROUPS + g + 1) * C_STATE].astype(BF16)
        grp = dict(bg=bg, cg=cg)
        groups.append(grp)
        for rr in range(C_REP):
            h = g * C_REP + rr
            sl = slice(h * C_HEAD, (h + 1) * C_HEAD)
            probs.append(dict(
                grp=grp, h=bi * C_HEADS + h, h_scr=h_scr, tri=tri,
                diff=cum_all[:, h * length:(h + 1) * length] - cum_t[h:h + 1, :],
                xdt=xdt_all[:, sl].astype(BF16), xw=xw_all[:, sl], ech=ech_all[:, sl], ecl=ecl[:, h:h + 1]))
    return groups, probs, (y_ref, bi)


def _ssd_solve(dirs):
    for groups, _, _ in dirs:
        for grp in groups:
            grp["cb"] = _nt_dot(grp["cg"], grp["bg"])
    probs = [p for _, ps, _ in dirs for p in ps]
    for p in probs:
        lmat = jnp.exp(jnp.where(p["tri"], p["diff"], -jnp.inf))
        p["m"] = (p["grp"]["cb"] * lmat).astype(BF16)
        p["hs"] = p["h_scr"][p["h"]]
    for p in probs:
        p["y"] = (jnp.dot(p["m"], p["xdt"], preferred_element_type=F32)
                  + _nt_dot(p["grp"]["cg"], p["hs"].astype(BF16)) * p["ech"])
    for p in probs:
        xw = p["xw"].T.astype(BF16)
        p["h_scr"][p["h"]] = p["hs"] * p["ecl"] + jnp.dot(xw, p["grp"]["bg"], preferred_element_type=F32)
    for _, ps, (y_ref, bi) in dirs:
        y_ref[bi] = jnp.concatenate([p["y"] for p in ps], axis=-1)


def _ssd_scan_kernel(xf_ref, smf_ref, smtf_ref, xb_ref, smb_ref, smtb_ref, prm_ref, prmt_ref,
                     yf_ref, yb_ref, hf_scr, hb_scr):
    @pl.when(pl.program_id(1) == 0)
    def _():
        hf_scr[...] = jnp.zeros_like(hf_scr)
        hb_scr[...] = jnp.zeros_like(hb_scr)

    prm, prm_t = prm_ref[...], prmt_ref[...]
    dirs = []
    for bi in range(xf_ref.shape[0]):
        dirs.append(_ssd_dir(xf_ref[bi], smf_ref[bi], smtf_ref[bi], 0, prm, prm_t, hf_scr, yf_ref, True, bi))
        dirs.append(_ssd_dir(xb_ref[bi], smb_ref[bi], smtb_ref[bi], C_HEADS, prm, prm_t, hb_scr, yb_ref, False, bi))
    _ssd_solve(dirs)


def _bwd_chunk(ncc, nc):
    return lambda s: jnp.where(s < ncc, ncc - 1 - s, nc - 1 - s + ncc)


def _ssd_scan(xact, small, small_t, dt_bias_f, a_log_f, dt_bias_b, a_log_b, n_ctx):
    bsz, n, cdim = xact.shape
    L = C_CHUNK
    nc, ncc = n // L, n_ctx // L
    bw = _bwd_chunk(ncc, nc)
    prm = jnp.zeros((8, 128), F32).at[0, :16].set(jnp.concatenate([dt_bias_f, dt_bias_b]))
    prm = prm.at[1, :16].set(jnp.concatenate([a_log_f, a_log_b]))
    nb = SSD_NB if bsz % SSD_NB == 0 else 1
    fx = lambda w: pl.BlockSpec((nb, L, w), lambda b, s: (b, s, 0))
    bx = lambda w: pl.BlockSpec((nb, L, w), lambda b, s: (b, bw(s), 0))
    ft = pl.BlockSpec((nb, 128, L), lambda b, s: (b, 0, s))
    bt = pl.BlockSpec((nb, 128, L), lambda b, s: (b, 0, bw(s)))
    return pl.pallas_call(
        _ssd_scan_kernel,
        grid=(bsz // nb, nc),
        in_specs=[fx(cdim), fx(128), ft, bx(cdim), bx(128), bt, _const_spec((8, 128)), _const_spec((128, 8))],
        out_specs=[fx(C_WIDTH), bx(C_WIDTH)],
        out_shape=[jax.ShapeDtypeStruct((bsz, n, C_WIDTH), F32)] * 2,
        scratch_shapes=[pltpu.VMEM((nb * C_HEADS, C_HEAD, C_STATE), F32)] * 2,
        compiler_params=_cp("parallel", "arbitrary"),
    )(xact, small, small_t, xact, small, small_t, prm, prm.T)


def _gla_solve(items):
    hk = D_HEADS * D_KEY
    for it in items:
        gd = it["sm"][:, 2 * C_HEADS:2 * C_HEADS + D_LORA]
        it["zz"] = jnp.dot(gd, it["up"], precision=HI, preferred_element_type=F32) + it["bias"]
    for it in items:
        lg = -_softplus(-it["zz"]) * (1.0 / D_TAU)
        it["tri"], _ = _scan_masks(lg.shape[0], it["causal"])
        it["cum"] = _cumsum3(it["tri"].astype(BF16), lg)
    probs = []
    for it in items:
        cum = it["cum"]
        last = cum.shape[0] - 1 if it["causal"] else 0
        cl = cum[last:last + 1, :]
        q = it["qk"][:, :hk] * (D_KEY ** -0.5)
        k = it["qk"][:, hk:]
        qe = (q * jnp.exp(cum)).astype(BF16)
        ke = (k * jnp.exp(-cum)).astype(BF16)
        kd = (k * jnp.exp(cl - cum)).astype(BF16)
        el = jnp.exp(cl)
        for h in range(D_HEADS):
            sl = slice(h * D_KEY, (h + 1) * D_KEY)
            probs.append(dict(it=it, h=h, qe=qe[:, sl], ke=ke[:, sl], kd=kd[:, sl], el=el[:, sl],
                              vh=it["v"][:, h * D_VAL:(h + 1) * D_VAL]))
    for p in probs:
        p["att"] = jnp.where(p["it"]["tri"], _nt_dot(p["qe"], p["ke"]), 0.0).astype(BF16)
        p["st"] = p["it"]["st_scr"][p["it"]["base"] + p["h"]]
    for p in probs:
        y = jnp.dot(p["att"], p["vh"].astype(BF16), preferred_element_type=F32)
        p["it"]["y_ref"][p["it"]["bi"], :, p["h"] * D_VAL:(p["h"] + 1) * D_VAL] = (
            y + _nt_dot(p["qe"], p["st"].astype(BF16)))
    for p in probs:
        p["it"]["st_scr"][p["it"]["base"] + p["h"]] = (
            p["st"] * p["el"] + jnp.dot(p["vh"].T.astype(BF16), p["kd"], preferred_element_type=F32))


def _gla_scan_kernel(qkf_ref, vf_ref, smf_ref, qkb_ref, vb_ref, smb_ref, upf_ref, upb_ref, bias_ref,
                     yf_ref, yb_ref, sf_scr, sb_scr):
    @pl.when(pl.program_id(1) == 0)
    def _():
        sf_scr[...] = jnp.zeros_like(sf_scr)
        sb_scr[...] = jnp.zeros_like(sb_scr)

    items = []
    for bi in range(qkf_ref.shape[0]):
        items.append(dict(qk=qkf_ref[bi], v=vf_ref[bi], sm=smf_ref[bi], up=upf_ref[...], bias=bias_ref[0:1, :],
                          causal=True, st_scr=sf_scr, base=bi * D_HEADS, y_ref=yf_ref, bi=bi))
        items.append(dict(qk=qkb_ref[bi], v=vb_ref[bi], sm=smb_ref[bi], up=upb_ref[...], bias=bias_ref[1:2, :],
                          causal=False, st_scr=sb_scr, base=bi * D_HEADS, y_ref=yb_ref, bi=bi))
    _gla_solve(items)


def _gla_scan(qk, v, small, gk_up_f, gk_b_f, gk_up_b, gk_b_b, n_ctx):
    bsz, n, _ = qk.shape
    L = D_CHUNK
    nc, ncc = n // L, n_ctx // L
    bw = _bwd_chunk(ncc, nc)
    hk = D_HEADS * D_KEY
    bias = jnp.zeros((8, hk), F32).at[0].set(gk_b_f).at[1].set(gk_b_b)
    nb = GLA_NB if bsz % GLA_NB == 0 else 1
    fx = lambda w: pl.BlockSpec((nb, L, w), lambda b, s: (b, s, 0))
    bx = lambda w: pl.BlockSpec((nb, L, w), lambda b, s: (b, bw(s), 0))
    return pl.pallas_call(
        _gla_scan_kernel,
        grid=(bsz // nb, nc),
        in_specs=[fx(2 * hk), fx(D_WIDTH), fx(128), bx(2 * hk), bx(D_WIDTH), bx(128),
                  _const_spec((D_LORA, hk)), _const_spec((D_LORA, hk)), _const_spec((8, hk))],
        out_specs=[fx(D_WIDTH), bx(D_WIDTH)],
        out_shape=[jax.ShapeDtypeStruct((bsz, n, D_WIDTH), F32)] * 2,
        scratch_shapes=[pltpu.VMEM((nb * D_HEADS, D_VAL, D_KEY), F32)] * 2,
        compiler_params=_cp("parallel", "arbitrary"),
    )(qk, v, small, qk, v, small, gk_up_f, gk_up_b, bias)


def kernel(x, c, ctx, c_ctx, l0_mod_w, l0_mod_b, l0_norm1, l0_norm2, l0_w_in, l0_w_out, l0_mlp_w1, l0_mlp_w2, l0_lam_q1, l0_lam_k1, l0_lam_q2, l0_lam_k2, l0_subln, l0_mu, l0_w0_f, l0_w2_f, l0_w0_b, l0_w2_b, l0_a0, l0_a2, l0_g2, l0_k_k, l0_k_a, l0_r_k, l0_lnx_w, l0_lnx_b, l1_mod_w, l1_mod_b, l1_norm1, l1_norm2, l1_w_in, l1_w_out, l1_mlp_w1, l1_mlp_w2, l1_conv_w, l1_conv_b, l1_dt_bias_f, l1_a_log_f, l1_dt_bias_b, l1_a_log_b, l1_d_skip, l1_ssm_norm, l1_gk_up_f, l1_gk_b_f, l1_gk_up_b, l1_gk_b_b, l1_gla_norm, final_norm):
    bsz, n_lat, d = x.shape
    n_ctx = ctx.shape[1]
    assert d == D and n_ctx % TM == 0 and n_lat % TM == 0 and n_lat % GRID_W == 0
    nct = n_ctx // TM
    xc = jnp.concatenate([ctx, x], axis=1)

    mod0 = _modulation(c, c_ctx, l0_mod_w, l0_mod_b)
    qkv, pb = _inproj0(xc, mod0, l0_norm1, l0_w_in, _rope_tables(n_ctx, n_lat), nct)
    lamv = jnp.stack([l0_lam_q1, l0_lam_k1, l0_lam_q2, l0_lam_k2], axis=0)
    oa = _attention(qkv, lamv, l0_subln, 0, n_ctx)
    yf_b, yb_b, bonus, gate = _rwkv_mix(pb, n_ctx, l0_mu, l0_w0_f, l0_w2_f, l0_w0_b, l0_w2_b, l0_a0, l0_a2, l0_g2,
                                        l0_k_k, l0_k_a, l0_r_k)
    xc = _post(xc, _mix_even, [oa, yf_b, yb_b, bonus, gate],
               [l0_lnx_w.reshape(1, -1), l0_lnx_b.reshape(1, -1), _block_ones(B_WIDTH, B_HEAD)],
               mod0, l0_w_out, l0_norm2, l0_mlp_w1, l0_mlp_w2, None, nct, False)

    mod1 = _modulation(c, c_ctx, l1_mod_w, l1_mod_b)
    z, xbc, qk, v, gg, small = _inproj1(xc, mod1, l1_norm1, l1_w_in, nct)
    xact = _conv(xbc, l1_conv_w, l1_conv_b, nct)
    yf_c, yb_c = _ssd_scan(xact, small, jnp.swapaxes(small, 1, 2), l1_dt_bias_f, l1_a_log_f,
                           l1_dt_bias_b, l1_a_log_b, n_ctx)
    yf_d, yb_d = _gla_scan(qk, v, small, l1_gk_up_f, l1_gk_b_f, l1_gk_up_b, l1_gk_b_b, n_ctx)
    return _post(xc, _mix_odd, [yf_c, yb_c, xact, z, yf_d, yb_d, gg],
                 [jnp.repeat(l1_d_skip, C_HEAD).reshape(1, C_WIDTH), l1_ssm_norm.reshape(1, C_WIDTH),
                  l1_gla_norm.reshape(1, D_VAL)],
                 mod1, l1_w_out, l1_norm2, l1_mlp_w1, l1_mlp_w2, final_norm, nct, True)
```

```python
import functools
import math

import jax
import jax.numpy as jnp
from jax import lax
from jax.experimental import pallas as pl
from jax.experimental.pallas import tpu as pltpu

F32 = jnp.float32
BF16 = jnp.bfloat16
HI = lax.Precision.HIGHEST

D = 1024
EPS = 1e-6
GRID_W = 64
ROPE_THETA = 10000.0
MLP_HIDDEN = 4 * D

A_HEADS, A_QK, A_V = 4, 64, 128
A_PROJ = 1536
A_SUBLN_EPS = 1e-5
B_HEADS, B_HEAD = 8, 64
B_WIDTH = 512
B_PROJ = 1792
B_GN_EPS = 64e-5
C_HEADS, C_HEAD, C_GROUPS, C_REP, C_STATE = 8, 64, 2, 4, 128
C_WIDTH = 512
C_CONV = 5
C_CHUNK = 128
C_CONV_DIM = 1024
D_HEADS, D_KEY, D_VAL = 4, 64, 128
D_WIDTH = 512
D_LORA = 16
D_TAU = 16.0
D_CHUNK = 64
ODD_PAD = 3200

TM = 256
TQ = 128
B_CHUNK = 64
B_GROUP = 4
ATTN_KC = 512
SSD_NB = 4
GLA_SAFE = 60.0
GLA_NB = 4
HALO = 8
VMEM_LIMIT = 56 * 1024 * 1024


def _cp(*sem):
    return pltpu.CompilerParams(dimension_semantics=sem, vmem_limit_bytes=VMEM_LIMIT)


def _const_spec(shape):
    nd = len(shape)
    return pl.BlockSpec(shape, lambda *_: (0,) * nd, pipeline_mode=pl.Buffered(1))


def _softplus(x):
    return jnp.maximum(x, 0.0) + jnp.log(1.0 + jnp.exp(-jnp.abs(x)))


def _sigmoid(x):
    return 1.0 / (1.0 + jnp.exp(-x))


def _silu(x):
    return x * _sigmoid(x)


def _ada(x, g, sh, sc):
    ms = jnp.mean(x * x, axis=-1, keepdims=True)
    return (x * lax.rsqrt(ms + EPS) * g) * (1.0 + sc) + sh


def _segsum(x, e, pieces=3):
    hi = x.astype(BF16)
    r1 = x - hi.astype(F32)
    mid = r1.astype(BF16)
    d = lambda a: jnp.dot(a, e, preferred_element_type=F32)
    if pieces == 2:
        return d(hi) + d(mid)
    lo = (r1 - mid.astype(F32)).astype(BF16)
    return d(hi) + d(mid) + d(lo)


def _block_ones(n, seg):
    i = jnp.arange(n) // seg
    return (i[:, None] == i[None, :]).astype(BF16)


def _tile_is_lat(nct):
    return lambda b, i: (b, (i >= nct).astype(jnp.int32), 0, 0)


def _mod_kernel(c_ref, w_ref, b_ref, o_ref):
    s = _silu(c_ref[...])
    o_ref[...] = jnp.dot(s, w_ref[...], precision=HI, preferred_element_type=F32) + b_ref[...]


def _modulation(c, c_ctx, mod_w, mod_b):
    bsz = c.shape[0]
    rows = 8 * ((bsz + 1 + 7) // 8)
    cs = jnp.zeros((rows, D), F32).at[0].set(c_ctx).at[1:bsz + 1].set(c)
    n6 = mod_w.shape[1]
    tn = 1024
    m = pl.pallas_call(
        _mod_kernel,
        grid=(n6 // tn,),
        in_specs=[pl.BlockSpec((rows, D), lambda j: (0, 0)),
                  pl.BlockSpec((D, tn), lambda j: (0, j)),
                  pl.BlockSpec((1, tn), lambda j: (0, j))],
        out_specs=pl.BlockSpec((rows, tn), lambda j: (0, j)),
        out_shape=jax.ShapeDtypeStruct((rows, n6), F32),
        compiler_params=_cp("arbitrary"),
    )(cs, mod_w, mod_b.reshape(1, n6))
    m_c = jnp.broadcast_to(m[0:1], (bsz, n6))
    return jnp.stack([m_c, m[1:bsz + 1]], axis=1)[:, :, None, :]


def _rope_tables(n_ctx, n_lat):
    t = jnp.arange(n_lat)
    row = (t // GRID_W).astype(F32)
    col = (t % GRID_W).astype(F32)
    nf = A_QK // 4
    inv = ROPE_THETA ** (-jnp.arange(nf, dtype=F32) / nf)
    ar = row[:, None] * inv[None, :]
    ac = col[:, None] * inv[None, :]
    z = jnp.zeros_like(ar)
    c64 = jnp.concatenate([jnp.cos(ar), jnp.cos(ar), jnp.cos(ac), jnp.cos(ac)], axis=-1)
    s1 = jnp.concatenate([-jnp.sin(ar), z, -jnp.sin(ac), z], axis=-1)
    s2 = jnp.concatenate([z, jnp.sin(ar), z, jnp.sin(ac)], axis=-1)
    pad = lambda a, v: jnp.concatenate([jnp.full((n_ctx, 128), v, F32), jnp.tile(a, (1, 2))], axis=0)
    return pad(c64, 1.0), pad(s1, 0.0), pad(s2, 0.0)


def _inproj0_kernel(x_ref, mod_ref, g_ref, wa_ref, wb_ref, c_ref, s1_ref, s2_ref, qkv_ref, pb_ref):
    mod = mod_ref[...]
    h = _ada(x_ref[...], g_ref[...], mod[:, 0:D], mod[:, D:2 * D]).astype(BF16)
    pa = jnp.dot(h, wa_ref[...], preferred_element_type=F32)
    c, s1, s2 = c_ref[...], s1_ref[...], s2_ref[...]
    scale = A_QK ** -0.5 * math.log2(math.e)
    for j in range(8):
        blk = pa[:, j * 128:(j + 1) * 128]
        rot = blk * c + pltpu.roll(blk, 112, 1) * s1 + pltpu.roll(blk, 16, 1) * s2
        if j < 4:
            rot = rot * scale
        qkv_ref[:, j * 128:(j + 1) * 128] = rot.astype(BF16)
    qkv_ref[:, 1024:1536] = pa[:, 1024:1536].astype(BF16)
    pb_ref[...] = jnp.dot(h, wb_ref[...], preferred_element_type=F32)


def _inproj0(xc, mod, norm1, w_in, tabs, nct):
    bsz, n, _ = xc.shape
    wa = w_in[:, :A_PROJ].astype(BF16)
    wb = w_in[:, A_PROJ:].astype(BF16)
    tok = lambda w: pl.BlockSpec((None, TM, w), lambda b, i: (b, i, 0))
    tab = pl.BlockSpec((TM, 128), lambda b, i: (i, 0))
    return pl.pallas_call(
        _inproj0_kernel,
        grid=(bsz, n // TM),
        in_specs=[tok(D), pl.BlockSpec((None, None, 1, 6 * D), _tile_is_lat(nct)),
                  _const_spec((1, D)), _const_spec((D, A_PROJ)), _const_spec((D, B_PROJ)),
                  tab, tab, tab],
        out_specs=[tok(A_PROJ), tok(B_PROJ)],
        out_shape=[jax.ShapeDtypeStruct((bsz, n, A_PROJ), BF16),
                   jax.ShapeDtypeStruct((bsz, n, B_PROJ), F32)],
        compiler_params=_cp("parallel", "parallel"),
    )(xc, mod, norm1.reshape(1, D), wa, wb, *tabs)


def _attn_kernel(lam_ref, q_ref, k_ref, v_ref, subln_ref, o_ref, vt_scr, s0_scr, s1_scr, m0_scr, m1_scr, *,
                 lam_init, n_ctx, nq_ctx, nq):
    qi = pl.program_id(2)
    n_all = k_ref.shape[0]

    @pl.when(qi == 0)
    def _():
        step = ATTN_KC
        for c0 in range(0, n_all, step):
            c1 = min(c0 + step, n_all)
            vt_scr[:, c0:c1] = v_ref[c0:c1, :].astype(F32).T.astype(BF16)

    lv = lam_ref[...]
    lam = (jnp.exp(jnp.sum(lv[0:1] * lv[1:2], axis=-1, keepdims=True))
           - jnp.exp(jnp.sum(lv[2:3] * lv[3:4], axis=-1, keepdims=True)) + lam_init)
    ts = q_ref.shape[0]

    def tree(vals, op):
        while len(vals) > 1:
            vals = [op(vals[i], vals[i + 1]) if i + 1 < len(vals) else vals[i] for i in range(0, len(vals), 2)]
        return vals[0]

    def stage(chunks_cur, chunks_prev, parity):
        s_cur, s_prev = (s0_scr, s1_scr) if parity == 0 else (s1_scr, s0_scr)
        m_cur, m_prev_scr = (m0_scr, m1_scr) if parity == 0 else (m1_scr, m0_scr)
        if chunks_cur:
            q = q_ref[...]
            lane = lax.broadcasted_iota(jnp.int32, q.shape, 1)
            zero = jnp.zeros_like(q)
            q2 = jnp.concatenate([jnp.where(lane < A_QK, q, zero), jnp.where(lane >= A_QK, q, zero)], axis=0)
        if chunks_prev:
            m_prev = m_prev_scr[0:1, :]
        ms, ls, accs = [], [], []
        for ci in range(max(len(chunks_cur), len(chunks_prev))):
            if ci < len(chunks_cur):
                c0, cn = chunks_cur[ci]
                s = _nt_dot(k_ref[c0:c0 + cn, :], q2)
                s_cur[c0:c0 + cn, :] = s
                ms.append(jnp.max(s, axis=0, keepdims=True))
            if ci < len(chunks_prev):
                c0, cn = chunks_prev[ci]
                p = jnp.exp2(s_prev[c0:c0 + cn, :] - m_prev)
                ls.append(jnp.sum(p, axis=0, keepdims=True))
                accs.append(jnp.dot(vt_scr[:, c0:c0 + cn], p.astype(BF16), preferred_element_type=F32))
        if chunks_cur:
            m_cur[...] = jnp.broadcast_to(tree(ms, jnp.maximum), m_cur.shape)
        if chunks_prev:
            on = tree(accs, jnp.add) / tree(ls, jnp.add)
            o = (on[:, :ts] - lam * on[:, ts:]).T
            msq = jnp.mean(o * o, axis=-1, keepdims=True)
            o_ref[...] = (o * lax.rsqrt(msq + A_SUBLN_EPS) * subln_ref[...]) * (1.0 - lam_init)
        else:
            o_ref[...] = jnp.zeros_like(o_ref)

    kc = min(ATTN_KC, n_all - n_ctx)
    ctx_chunks = [(0, n_ctx)]
    all_chunks = ctx_chunks + [(n_ctx + j * kc, kc) for j in range((n_all - n_ctx) // kc)]
    for parity in (0, 1):
        on = lambda cond: pl.when(jnp.logical_and(cond, qi % 2 == parity))
        if parity == 0:
            on(qi == 0)(functools.partial(stage, ctx_chunks, [], parity))
        if nq_ctx > 1:
            on(jnp.logical_and(qi >= 1, qi < nq_ctx))(functools.partial(stage, ctx_chunks, ctx_chunks, parity))
        if nq_ctx % 2 == parity:
            on(qi == nq_ctx)(functools.partial(stage, all_chunks, ctx_chunks, parity))
        on(jnp.logical_and(qi > nq_ctx, qi < nq))(functools.partial(stage, all_chunks, all_chunks, parity))
        if nq % 2 == parity:
            on(qi == nq)(functools.partial(stage, [], all_chunks, parity))


def _attention(qkv, lamv, subln, layer_idx, n_ctx):
    bsz, n, _ = qkv.shape
    lam_init = 0.8 - 0.6 * math.exp(-0.3 * layer_idx)
    nq = n // TQ
    assert n_ctx % TQ == 0 and n_ctx < n
    kern = functools.partial(_attn_kernel, lam_init=lam_init, n_ctx=n_ctx, nq_ctx=n_ctx // TQ, nq=nq)
    return pl.pallas_call(
        kern,
        grid=(bsz, A_HEADS, nq + 1),
        in_specs=[_const_spec((4, A_QK)),
                  pl.BlockSpec((None, TQ, 128), lambda b, h, i: (b, jnp.minimum(i, nq - 1), h)),
                  pl.BlockSpec((None, n, 128), lambda b, h, i: (b, 0, A_HEADS + h)),
                  pl.BlockSpec((None, n, 128), lambda b, h, i: (b, 0, 2 * A_HEADS + h)),
                  _const_spec((1, A_V))],
        out_specs=pl.BlockSpec((None, TQ, 128), lambda b, h, i: (b, jnp.maximum(i - 1, 0), h)),
        out_shape=jax.ShapeDtypeStruct((bsz, n, A_HEADS * A_V), F32),
        scratch_shapes=[pltpu.VMEM((A_V, n), BF16), pltpu.VMEM((n, 2 * TQ), F32), pltpu.VMEM((n, 2 * TQ), F32),
                        pltpu.VMEM((8, 2 * TQ), F32), pltpu.VMEM((8, 2 * TQ), F32)],
        compiler_params=_cp("parallel", "parallel", "arbitrary"),
    )(lamv, qkv, qkv, qkv, subln.reshape(1, A_V))


def _halo_flags(i, nct, nt):
    prev_ok = jnp.logical_and(i != 0, i != nct)
    next_ok = jnp.logical_and(i != nct - 1, i != nt - 1)
    return prev_ok, next_ok


def _rwkv_prep_kernel(p_ref, pp_ref, pn_ref, mu_ref, vec_ref, w2f_ref, w2b_ref, a2_ref, g2_ref, e_ref,
                      r_ref, wf_ref, wb_ref, k_ref, v_ref, kk_ref, kb_ref, bonus_ref, g_ref, *, nct, nt):
    i = pl.program_id(1)
    f = p_ref[...]
    tm = f.shape[0]
    prev_ok, next_ok = _halo_flags(i, nct, nt)
    hp = jnp.where(prev_ok, pp_ref[HALO - 1:HALO, :], 0.0)
    hn = jnp.where(next_ok, pn_ref[0:1, :], 0.0)
    row = lax.broadcasted_iota(jnp.int32, f.shape, 0)
    prev = jnp.where(row == 0, hp, pltpu.roll(f, 1, 0))
    nxt = jnp.where(row == tm - 1, hn, pltpu.roll(f, tm - 1, 0))
    fm = f + (0.5 * (prev + nxt) - f) * mu_ref[...]
    r = fm[:, 0:512]
    k = fm[:, 512:1024]
    v = fm[:, 1024:1536]
    wd = fm[:, 1536:1600]
    ad = fm[:, 1600:1664]
    gd = fm[:, 1664:1792]
    w0f, w0b, a0 = vec_ref[0:1, :], vec_ref[1:2, :], vec_ref[2:3, :]
    k_k, k_a, r_k = vec_ref[3:4, :], vec_ref[4:5, :], vec_ref[5:6, :]
    e = e_ref[...]
    mm = lambda a, w_ref: jnp.dot(a.astype(BF16), w_ref[...], preferred_element_type=F32)
    a = _sigmoid(a0 + mm(ad, a2_ref))
    tw = jnp.tanh(wd)
    decay = lambda w0, w2_ref: -jnp.exp(-_softplus(-(w0 + mm(tw, w2_ref))) - 0.5)
    kkr = k * k_k
    kk = kkr / jnp.maximum(jnp.sqrt(_segsum(kkr * kkr, e, 2)), 1e-12)
    k2 = k * (1.0 + (a - 1.0) * k_a)
    r_ref[...] = r
    wf_ref[...] = decay(w0f, w2f_ref)
    wb_ref[...] = decay(w0b, w2b_ref)
    k_ref[...] = k2
    v_ref[...] = v
    kk_ref[...] = kk
    kb_ref[...] = kk * a
    bonus_ref[...] = _segsum(r * k2 * r_k, e, 2) * v
    g_ref[...] = mm(_sigmoid(gd), g2_ref)


def _rwkv_prep(pb, mu, vecs, w2_f, w2_b, a2, g2, nct):
    bsz, n, _ = pb.shape
    nt = n // TM
    hb = TM // HALO
    tok = lambda w: pl.BlockSpec((None, TM, w), lambda b, i: (b, i, 0))
    kern = functools.partial(_rwkv_prep_kernel, nct=nct, nt=nt)
    outs = pl.pallas_call(
        kern,
        grid=(bsz, nt),
        in_specs=[tok(B_PROJ),
                  pl.BlockSpec((None, HALO, B_PROJ), lambda b, i: (b, jnp.maximum(i * hb - 1, 0), 0)),
                  pl.BlockSpec((None, HALO, B_PROJ), lambda b, i: (b, jnp.minimum((i + 1) * hb, nt * hb - 1), 0)),
                  _const_spec((1, B_PROJ)), _const_spec((8, B_WIDTH)),
                  _const_spec((64, B_WIDTH)), _const_spec((64, B_WIDTH)), _const_spec((64, B_WIDTH)),
                  _const_spec((128, B_WIDTH)), _const_spec((B_WIDTH, B_WIDTH))],
        out_specs=[tok(B_WIDTH)] * 9,
        out_shape=[jax.ShapeDtypeStruct((bsz, n, B_WIDTH), F32)] * 9,
        compiler_params=_cp("parallel", "parallel"),
    )(pb, pb, pb, mu.reshape(1, B_PROJ), vecs, w2_f.astype(BF16), w2_b.astype(BF16), a2.astype(BF16),
      g2.astype(BF16), _block_ones(B_WIDTH, B_HEAD))
    return outs


def _cumsum3(tri_b, x):
    hi = x.astype(BF16)
    r1 = x - hi.astype(F32)
    mid = r1.astype(BF16)
    lo = (r1 - mid.astype(F32)).astype(BF16)
    d = lambda a: jnp.dot(tri_b, a, preferred_element_type=F32)
    return d(hi) + d(mid) + d(lo)


def _head_spread(heads, width):
    row = lax.broadcasted_iota(jnp.int32, (heads, heads * width), 0)
    col = lax.broadcasted_iota(jnp.int32, (heads, heads * width), 1)
    return (row == col // width).astype(BF16)


def _bdot(a, b):
    return jnp.dot(a.astype(BF16), b.astype(BF16), preferred_element_type=F32)


def _rwkv_dir(r, lw, k, v, kk, kb, s_scr, y_ref, causal):
    length = r.shape[0]
    assert length == B_HEAD
    gw = B_GROUP * B_HEAD
    ii = lax.broadcasted_iota(jnp.int32, (length, gw), 0)
    jj = lax.broadcasted_iota(jnp.int32, (length, gw), 1) % length
    tri = (ii >= jj) if causal else (ii <= jj)
    strict = jnp.logical_and(tri, ii != jj)
    lvl = []
    m = 1
    while m < length:
        same2 = (ii // (2 * m)) == (jj // (2 * m))
        diff1 = (ii // m) != (jj // m)
        lvl.append(jnp.logical_and(jnp.logical_and(same2, diff1), strict))
        m *= 2
    tri_sq, _ = _scan_masks(length, causal)
    cum = _cumsum3(tri_sq.astype(BF16), lw)
    last = length - 1 if causal else 0
    cl = cum[last:last + 1, :]
    gi = jnp.exp(-cum)
    gl = jnp.exp(cl - cum)
    kg = kk * jnp.exp(cum - lw)
    rg = r * jnp.exp(cum)
    ki, bi = k * gi, kb * gi
    kd, bd = k * gl, kb * gl
    el = jnp.exp(cl)
    probs = []
    for g in range(r.shape[1] // gw):
        sl = slice(g * gw, (g + 1) * gw)
        probs.append(dict(
            sl=sl, tri=tri, strict=strict, lvl=lvl, s_scr=s_scr, y_ref=y_ref, g=g,
            lhs=jnp.concatenate([kg[:, sl], rg[:, sl]], axis=0).astype(BF16),
            bi=bi[:, sl], ki=ki[:, sl], v=v[:, sl],
            kbd=jnp.concatenate([kd[:, sl], -bd[:, sl]], axis=0).astype(BF16),
            el=el[:, sl]))
    return probs


def _rwkv_solve(probs):
    length = probs[0]["v"].shape[0]
    gw = probs[0]["v"].shape[1]
    hrow = lax.broadcasted_iota(jnp.int32, (gw, gw), 0) // B_HEAD
    hcol = lax.broadcasted_iota(jnp.int32, (gw, gw), 1) // B_HEAD
    same_head = hrow == hcol
    same_head_b = same_head.astype(BF16)

    def bd(x):
        xb = x.astype(BF16)
        return jnp.concatenate([xb] * B_GROUP, axis=0) * same_head_b

    mm = lambda a, w: jnp.dot(a.astype(BF16), w, preferred_element_type=F32)
    for p in probs:
        p["sc_b"] = _nt_dot(p["lhs"], bd(p["bi"]))
        p["sc_k"] = _nt_dot(p["lhs"], bd(p["ki"]))
    for p in probs:
        p["a_kb"] = jnp.where(p["strict"], p["sc_b"][:length], 0.0)
        p["a_kk"] = jnp.where(p["strict"], p["sc_k"][:length], 0.0).astype(BF16)
        p["a_rb"] = jnp.where(p["tri"], p["sc_b"][length:], 0.0).astype(BF16)
        p["a_rk"] = jnp.where(p["tri"], p["sc_k"][length:], 0.0).astype(BF16)
        p["nn"] = -jnp.where(p["lvl"][0], p["a_kb"], 0.0)
        p["s0"] = p["s_scr"][p["g"]]
        p["vbd"] = bd(p["v"])
    for p in probs:
        p["hs"] = _nt_dot(p["lhs"], bd(p["s0"]))
        p["akv"] = mm(p["a_kk"], p["vbd"])
    for lv in range(1, len(probs[0]["lvl"])):
        for p in probs:
            aoff = jnp.where(p["lvl"][lv], p["a_kb"], 0.0)
            p["x"] = aoff + mm(p["nn"], bd(aoff))
        for p in probs:
            p["nn"] = p["nn"] - (p["x"] + mm(p["x"], bd(p["nn"])))
    for p in probs:
        rhs = p["hs"][:length] + p["akv"]
        p["u"] = rhs + mm(p["nn"], bd(rhs))
    for p in probs:
        p["y_ref"][:, p["sl"]] = p["hs"][length:] + mm(p["a_rk"], p["vbd"]) - mm(p["a_rb"], bd(p["u"]))
    for p in probs:
        vu = jnp.concatenate([p["v"], p["u"]], axis=0)
        z = jnp.dot(vu.T.astype(BF16), p["kbd"], preferred_element_type=F32)
        z = jnp.where(same_head, z, 0.0)
        fold = z[0:B_HEAD]
        for j in range(1, B_GROUP):
            fold = fold + z[j * B_HEAD:(j + 1) * B_HEAD]
        p["s_scr"][p["g"]] = p["s0"] * p["el"] + fold


def _rwkv_chunk_kernel(*refs):
    fwd, bwd = refs[0:6], refs[6:12]
    yf_ref, yb_ref, sf_scr, sb_scr = refs[12:16]

    @pl.when(pl.program_id(1) == 0)
    def _():
        sf_scr[...] = jnp.zeros_like(sf_scr)
        sb_scr[...] = jnp.zeros_like(sb_scr)

    probs = _rwkv_dir(*[x[...] for x in fwd], sf_scr, yf_ref, True)
    probs += _rwkv_dir(*[x[...] for x in bwd], sb_scr, yb_ref, False)
    _rwkv_solve(probs)


def _rwkv_chunk_scan(r, lwf, lwb, k, v, kk, kb, n_ctx):
    bsz, n, width = r.shape
    L = B_CHUNK
    nc, ncc = n // L, n_ctx // L
    bw = _bwd_chunk(ncc, nc)
    fx = pl.BlockSpec((None, L, width), lambda b, s: (b, s, 0))
    bx = pl.BlockSpec((None, L, width), lambda b, s: (b, bw(s), 0))
    return pl.pallas_call(
        _rwkv_chunk_kernel,
        grid=(bsz, nc),
        in_specs=[fx] * 6 + [bx] * 6,
        out_specs=[fx, bx],
        out_shape=[jax.ShapeDtypeStruct((bsz, n, width), F32)] * 2,
        scratch_shapes=[pltpu.VMEM((B_HEADS // B_GROUP, B_HEAD, B_GROUP * B_HEAD), F32)] * 2,
        compiler_params=_cp("parallel", "arbitrary"),
    )(r, lwf, k, v, kk, kb, r, lwb, k, v, kk, kb)


def _rwkv_mix(pb, n_ctx, mu, w0_f, w2_f, w0_b, w2_b, a0, a2, g2, k_k, k_a, r_k):
    zero = jnp.zeros_like(w0_f)
    vecs = jnp.stack([w0_f, w0_b, a0, k_k, k_a, r_k, zero, zero], axis=0)
    r, lwf, lwb, k2, v, kk, kb, bonus, g = _rwkv_prep(pb, mu, vecs, w2_f, w2_b, a2, g2, n_ctx // TM)
    yf, yb = _rwkv_chunk_scan(r, lwf, lwb, k2, v, kk, kb, n_ctx)
    return yf, yb, bonus, g


def _mix_even(tok, cst):
    oa, yf, yb, bonus, g = tok
    lnw, lnb, e = cst
    y = yf + yb
    d = y - _segsum(y, e, 2) * (1.0 / B_HEAD)
    var = _segsum(d * d, e, 2) * (1.0 / B_HEAD)
    return oa,(d * lax.rsqrt(var + B_GN_EPS) * lnw + lnb + bonus) * g


def _mix_odd(tok, cst):
    yfc, ybc, xs, z, yfd, ybd, gg = tok
    dsk, ngc, ngd = cst
    y = (yfc + ybc + xs * dsk) * _silu(z)
    gw = C_WIDTH // C_GROUPS
    oc = []
    for g in range(C_GROUPS):
        blk = y[:, g * gw:(g + 1) * gw]
        ms = jnp.mean(blk * blk, axis=-1, keepdims=True)
        oc.append(blk * lax.rsqrt(ms + EPS) * ngc[:, g * gw:(g + 1) * gw])
    y = yfd + ybd
    gate = _silu(gg)
    od = []
    for h in range(D_HEADS):
        sl = slice(h * D_VAL, (h + 1) * D_VAL)
        blk = y[:, sl]
        ms = jnp.mean(blk * blk, axis=-1, keepdims=True)
        od.append(blk * lax.rsqrt(ms + EPS) * ngd * gate[:, sl])
    return jnp.concatenate(oc, axis=-1), jnp.concatenate(od, axis=-1)


def _post_kernel(*refs, mix_fn, n_tok, n_cst, final):
    x_ref = refs[0]
    tok = [r[...] for r in refs[1:1 + n_tok]]
    cst = [r[...] for r in refs[1 + n_tok:1 + n_tok + n_cst]]
    mod_ref, woa_ref, wob_ref, n2_ref, w1_ref, w2_ref, fn_ref, out_ref = refs[1 + n_tok + n_cst:]
    mod = mod_ref[...]
    g1, sh2 = mod[:, 2 * D:3 * D], mod[:, 3 * D:4 * D]
    sc2, g2 = mod[:, 4 * D:5 * D], mod[:, 5 * D:6 * D]
    o1, o2 = mix_fn(tok, cst)
    mix = (jnp.dot(o1.astype(BF16), woa_ref[...], preferred_element_type=F32)
           + jnp.dot(o2.astype(BF16), wob_ref[...], preferred_element_type=F32))
    x1 = x_ref[...] + g1 * mix
    h = _ada(x1, n2_ref[...], sh2, sc2).astype(BF16)
    hid = jnp.maximum(jnp.dot(h, w1_ref[...], preferred_element_type=F32), 0.0)
    hid = (hid * hid).astype(BF16)
    x2 = x1 + g2 * jnp.dot(hid, w2_ref[...], preferred_element_type=F32)
    if final:
        ms = jnp.mean(x2 * x2, axis=-1, keepdims=True)
        x2 = x2 * lax.rsqrt(ms + EPS) * fn_ref[...]
    out_ref[...] = x2


def _post(xc, mix_fn, tok_in, cst_in, mod, w_out, norm2, mlp_w1, mlp_w2, final_norm, nct, lat_only):
    bsz, n, _ = xc.shape
    half = w_out.shape[0] // 2
    t0 = nct if lat_only else 0
    nt = n // TM - t0
    tok = lambda w: pl.BlockSpec((None, TM, w), lambda b, i: (b, i + t0, 0))
    final = final_norm is not None
    fn = (final_norm if final else jnp.ones((D,), F32)).reshape(1, D)
    kern = functools.partial(_post_kernel, mix_fn=mix_fn, n_tok=len(tok_in), n_cst=len(cst_in), final=final)
    return pl.pallas_call(
        kern,
        grid=(bsz, nt),
        in_specs=[tok(D)] + [tok(half)] * len(tok_in) + [_const_spec(a.shape) for a in cst_in]
                 + [pl.BlockSpec((None, None, 1, 6 * D), lambda b, i: (b, (i + t0 >= nct).astype(jnp.int32), 0, 0)),
                    _const_spec((half, D)), _const_spec((half, D)), _const_spec((1, D)),
                    _const_spec((D, MLP_HIDDEN)), _const_spec((MLP_HIDDEN, D)), _const_spec((1, D))],
        out_specs=pl.BlockSpec((None, TM, D), lambda b, i: (b, i, 0)),
        out_shape=jax.ShapeDtypeStruct((bsz, nt * TM, D), F32),
        compiler_params=_cp("parallel", "parallel"),
    )(xc, *tok_in, *cst_in, mod, w_out[:half].astype(BF16), w_out[half:].astype(BF16), norm2.reshape(1, D),
      mlp_w1.astype(BF16), mlp_w2.astype(BF16), fn)


def _inproj1_kernel(x_ref, xp_ref, xn_ref, mod_ref, g_ref, w_ref, cw_ref, cb_ref,
                    z_ref, xact_ref, qk_ref, v_ref, gg_ref, sm_ref, *, nct, nt):
    mod = mod_ref[...]
    tm = x_ref.shape[0]
    ne = tm + 2 * HALO
    x_ext = jnp.concatenate([xp_ref[...], x_ref[...], xn_ref[...]], axis=0)
    h = _ada(x_ext, g_ref[...], mod[:, 0:D], mod[:, D:2 * D]).astype(BF16)
    xbc_ext = jnp.dot(h, w_ref[:, 512:1536], preferred_element_type=F32)
    h_main = h[HALO:HALO + tm]
    for ref, c0, c1 in ((z_ref, 0, 512), (qk_ref, 1536, 2048), (v_ref, 2048, 2560), (gg_ref, 2560, 3072),
                        (sm_ref, 3072, 3200)):
        ref[...] = jnp.dot(h_main, w_ref[:, c0:c1], preferred_element_type=F32)
    xbc = xbc_ext[HALO:HALO + tm]
    prev_ok, next_ok = _halo_flags(pl.program_id(1), nct, nt)
    row = lax.broadcasted_iota(jnp.int32, (ne, C_CONV_DIM), 0)
    dead = jnp.logical_or(jnp.logical_and(row < HALO, jnp.logical_not(prev_ok)),
                          jnp.logical_and(row >= HALO + tm, jnp.logical_not(next_ok)))
    ext = jnp.where(dead, 0.0, xbc_ext)
    acc = cb_ref[...] + cw_ref[2:3, :] * xbc
    for j in (0, 1, 3, 4):
        shifted = pltpu.roll(ext, (C_CONV // 2 - j) % ne, 0)[HALO:HALO + tm]
        acc = acc + cw_ref[j:j + 1, :] * shifted
    xact_ref[...] = _silu(acc)


def _halo_specs(width, nt):
    hb = TM // HALO
    return [pl.BlockSpec((None, HALO, width), lambda b, i: (b, jnp.maximum(i * hb - 1, 0), 0)),
            pl.BlockSpec((None, HALO, width), lambda b, i: (b, jnp.minimum((i + 1) * hb, nt * hb - 1), 0))]


def _inproj1(xc, mod, norm1, w_in, conv_w, conv_b, nct):
    bsz, n, _ = xc.shape
    nt = n // TM
    w8 = jnp.zeros((8, C_CONV_DIM), F32).at[:C_CONV].set(conv_w)
    c0 = C_WIDTH + C_CONV_DIM
    d0 = c0 + 2 * C_HEADS
    qk_w = 2 * D_HEADS * D_KEY
    w = jnp.concatenate([w_in[:, :c0], w_in[:, d0:d0 + qk_w + 2 * D_WIDTH], w_in[:, c0:d0],
                         w_in[:, d0 + qk_w + 2 * D_WIDTH:],
                         jnp.zeros((D, 128 - 2 * C_HEADS - D_LORA), F32)], axis=1).astype(BF16)
    tok = lambda wd: pl.BlockSpec((None, TM, wd), lambda b, i: (b, i, 0))
    widths = [512, 1024, 512, 512, 512, 128]
    return pl.pallas_call(
        functools.partial(_inproj1_kernel, nct=nct, nt=nt),
        grid=(bsz, nt),
        in_specs=[tok(D)] + _halo_specs(D, nt)
                 + [pl.BlockSpec((None, None, 1, 6 * D), _tile_is_lat(nct)),
                    _const_spec((1, D)), _const_spec((D, ODD_PAD)),
                    _const_spec((8, C_CONV_DIM)), _const_spec((1, C_CONV_DIM))],
        out_specs=[tok(wd) for wd in widths],
        out_shape=[jax.ShapeDtypeStruct((bsz, n, wd), F32) for wd in widths],
        compiler_params=_cp("parallel", "parallel"),
    )(xc, xc, xc, mod, norm1.reshape(1, D), w, w8, conv_b.reshape(1, C_CONV_DIM))


def _scan_masks(length, causal):
    ii = lax.broadcasted_iota(jnp.int32, (length, length), 0)
    jj = lax.broadcasted_iota(jnp.int32, (length, length), 1)
    tri = (ii >= jj) if causal else (ii <= jj)
    tri_t = (ii <= jj) if causal else (ii >= jj)
    return tri, tri_t


def _nt_dot(a, b):
    return lax.dot_general(a, b, (((1,), (1,)), ((), ())), preferred_element_type=F32)


def _ssd_dir(xa, sm, sm_t, col0, prm, prm_t, h_scr, y_ref, causal, bi):
    length = xa.shape[0]
    dtb, alog = prm[0:1, col0:col0 + C_HEADS], prm[1:2, col0:col0 + C_HEADS]
    dtb_t, alog_t = prm_t[col0:col0 + C_HEADS, 0:1], prm_t[col0:col0 + C_HEADS, 1:2]
    dt = _softplus(sm[:, col0:col0 + C_HEADS] + dtb)
    la = dt * (-jnp.exp(alog))
    la_t = _softplus(sm_t[col0:col0 + C_HEADS, :] + dtb_t) * (-jnp.exp(alog_t))
    tri, tri_t = _scan_masks(length, causal)
    cum = jnp.dot(tri.astype(F32), la, precision=HI, preferred_element_type=F32)
    cum_t = jnp.dot(la_t, tri_t.astype(F32), precision=HI, preferred_element_type=F32)
    last = length - 1 if causal else 0
    ecl = jnp.exp(cum[last:last + 1, :])
    spread = lambda a, w: _segsum(a, _head_spread(C_HEADS, w))
    xdt_all = xa[:, :C_WIDTH] * spread(dt, C_HEAD)
    xw_all = xdt_all * spread(jnp.exp(cum[last:last + 1, :] - cum), C_HEAD)
    ech_all = spread(jnp.exp(cum), C_HEAD)
    cum_all = spread(cum, length)
    groups, probs = [], []
    for g in range(C_GROUPS):
        bg = xa[:, C_WIDTH + g * C_STATE:C_WIDTH + (g + 1) * C_STATE].astype(BF16)
        cg = xa[:, C_WIDTH + (C_GROUPS + g) * C_STATE:C_WIDTH + (C_GROUPS + g + 1) * C_STATE].astype(BF16)
        grp = dict(bg=bg, cg=cg)
        groups.append(grp)
        for rr in range(C_REP):
            h = g * C_REP + rr
            sl = slice(h * C_HEAD, (h + 1) * C_HEAD)
            probs.append(dict(
                grp=grp, h=bi * C_HEADS + h, h_scr=h_scr, tri=tri,
                diff=cum_all[:, h * length:(h + 1) * length] - cum_t[h:h + 1, :],
                xdt=xdt_all[:, sl].astype(BF16), xw=xw_all[:, sl], ech=ech_all[:, sl], ecl=ecl[:, h:h + 1]))
    return groups, probs, (y_ref, bi)


def _ssd_solve(dirs):
    for groups, _, _ in dirs:
        for grp in groups:
            grp["cb"] = _nt_dot(grp["cg"], grp["bg"])
    probs = [p for _, ps, _ in dirs for p in ps]
    for p in probs:
        lmat = jnp.exp(jnp.where(p["tri"], p["diff"], -jnp.inf))
        p["m"] = (p["grp"]["cb"] * lmat).astype(BF16)
        p["hs"] = p["h_scr"][p["h"]]
    for p in probs:
        p["y"] = (jnp.dot(p["m"], p["xdt"], preferred_element_type=F32)
                  + _nt_dot(p["grp"]["cg"], p["hs"].astype(BF16)) * p["ech"])
    for p in probs:
        xw = p["xw"].T.astype(BF16)
        p["h_scr"][p["h"]] = p["hs"] * p["ecl"] + jnp.dot(xw, p["grp"]["bg"], preferred_element_type=F32)
    for _, ps, (y_ref, bi) in dirs:
        y_ref[bi] = jnp.concatenate([p["y"] for p in ps], axis=-1)


def _ssd_scan_kernel(xf_ref, smf_ref, smtf_ref, xb_ref, smb_ref, smtb_ref, prm_ref, prmt_ref,
                     yf_ref, yb_ref, hf_scr, hb_scr):
    @pl.when(pl.program_id(1) == 0)
    def _():
        hf_scr[...] = jnp.zeros_like(hf_scr)
        hb_scr[...] = jnp.zeros_like(hb_scr)

    prm, prm_t = prm_ref[...], prmt_ref[...]
    dirs = []
    for bi in range(xf_ref.shape[0]):
        dirs.append(_ssd_dir(xf_ref[bi], smf_ref[bi], smtf_ref[bi], 0, prm, prm_t, hf_scr, yf_ref, True, bi))
        dirs.append(_ssd_dir(xb_ref[bi], smb_ref[bi], smtb_ref[bi], C_HEADS, prm, prm_t, hb_scr, yb_ref, False, bi))
    _ssd_solve(dirs)


def _bwd_chunk(ncc, nc):
    return lambda s: jnp.where(s < ncc, ncc - 1 - s, nc - 1 - s + ncc)


def _ssd_scan(xact, small, small_t, dt_bias_f, a_log_f, dt_bias_b, a_log_b, n_ctx):
    bsz, n, cdim = xact.shape
    L = C_CHUNK
    nc, ncc = n // L, n_ctx // L
    bw = _bwd_chunk(ncc, nc)
    prm = jnp.zeros((8, 128), F32).at[0, :16].set(jnp.concatenate([dt_bias_f, dt_bias_b]))
    prm = prm.at[1, :16].set(jnp.concatenate([a_log_f, a_log_b]))
    nb = SSD_NB if bsz % SSD_NB == 0 else 1
    fx = lambda w: pl.BlockSpec((nb, L, w), lambda b, s: (b, s, 0))
    bx = lambda w: pl.BlockSpec((nb, L, w), lambda b, s: (b, bw(s), 0))
    ft = pl.BlockSpec((nb, 128, L), lambda b, s: (b, 0, s))
    bt = pl.BlockSpec((nb, 128, L), lambda b, s: (b, 0, bw(s)))
    return pl.pallas_call(
        _ssd_scan_kernel,
        grid=(bsz // nb, nc),
        in_specs=[fx(cdim), fx(128), ft, bx(cdim), bx(128), bt, _const_spec((8, 128)), _const_spec((128, 8))],
        out_specs=[fx(C_WIDTH), bx(C_WIDTH)],
        out_shape=[jax.ShapeDtypeStruct((bsz, n, C_WIDTH), F32)] * 2,
        scratch_shapes=[pltpu.VMEM((nb * C_HEADS, C_HEAD, C_STATE), F32)] * 2,
        compiler_params=_cp("parallel", "arbitrary"),
    )(xact, small, small_t, xact, small, small_t, prm, prm.T)


def _gla_gates(items):
    for it in items:
        gd = it["sm"][:, 2 * C_HEADS:2 * C_HEADS + D_LORA]
        it["zz"] = jnp.dot(gd, it["up"], precision=HI, preferred_element_type=F32) + it["bias"]
    for it in items:
        it["lg"] = -_softplus(-it["zz"]) * (1.0 / D_TAU)
        it["tri"], _ = _scan_masks(it["lg"].shape[0], it["causal"])
        it["cum"] = _cumsum3(it["tri"].astype(BF16), it["lg"])


def _gla_tokenwise(items, lg_scr):
    length = lg_scr.shape[1]
    hk = D_HEADS * D_KEY
    eye = (lax.broadcasted_iota(jnp.int32, (D_VAL, D_VAL), 0)
           == lax.broadcasted_iota(jnp.int32, (D_VAL, D_VAL), 1)).astype(F32)
    for ii, it in enumerate(items):
        lg_scr[ii] = it["lg"]

    rows = 8

    def body(step, carry):
        for ii, it in enumerate(items):
            grp = step if it["causal"] else length // rows - 1 - step
            r0 = pl.multiple_of(grp * rows, rows)
            g_blk = jnp.exp(lg_scr[ii, pl.ds(r0, rows), :])
            qk_blk = it["qk_ref"][it["bi"], pl.ds(r0, rows), :]
            v_blk = it["v_ref"][it["bi"], pl.ds(r0, rows), :]
            for h in range(D_HEADS):
                sl = slice(h * D_KEY, (h + 1) * D_KEY)
                kl = slice(hk + h * D_KEY, hk + (h + 1) * D_KEY)
                vl = slice(h * D_VAL, (h + 1) * D_VAL)
                st = it["st_scr"][it["base"] + h]
                ys = [None] * rows
                for r in (range(rows) if it["causal"] else range(rows - 1, -1, -1)):
                    v_col = jnp.sum(eye * v_blk[r:r + 1, vl], axis=1, keepdims=True)
                    st = st * g_blk[r:r + 1, sl] + v_col * qk_blk[r:r + 1, kl]
                    y_col = jnp.sum(st * (qk_blk[r:r + 1, sl] * (D_KEY ** -0.5)), axis=1, keepdims=True)
                    ys[r] = jnp.sum(eye * y_col, axis=0, keepdims=True)
                it["st_scr"][it["base"] + h] = st
                it["y_ref"][it["bi"], pl.ds(r0, rows), vl] = jnp.concatenate(ys, axis=0)
        return carry

    lax.fori_loop(0, length // rows, body, 0)


def _gla_solve(items):
    hk = D_HEADS * D_KEY
    probs = []
    for it in items:
        cum = it["cum"]
        last = cum.shape[0] - 1 if it["causal"] else 0
        cl = cum[last:last + 1, :]
        q = it["qk"][:, :hk] * (D_KEY ** -0.5)
        k = it["qk"][:, hk:]
        qe = (q * jnp.exp(cum)).astype(BF16)
        ke = (k * jnp.exp(-cum)).astype(BF16)
        kd = (k * jnp.exp(cl - cum)).astype(BF16)
        el = jnp.exp(cl)
        for h in range(D_HEADS):
            sl = slice(h * D_KEY, (h + 1) * D_KEY)
            probs.append(dict(it=it, h=h, qe=qe[:, sl], ke=ke[:, sl], kd=kd[:, sl], el=el[:, sl],
                              vh=it["v"][:, h * D_VAL:(h + 1) * D_VAL]))
    for p in probs:
        p["att"] = jnp.where(p["it"]["tri"], _nt_dot(p["qe"], p["ke"]), 0.0).astype(BF16)
        p["st"] = p["it"]["st_scr"][p["it"]["base"] + p["h"]]
    for p in probs:
        y = jnp.dot(p["att"], p["vh"].astype(BF16), preferred_element_type=F32)
        p["it"]["y_ref"][p["it"]["bi"], :, p["h"] * D_VAL:(p["h"] + 1) * D_VAL] = (
            y + _nt_dot(p["qe"], p["st"].astype(BF16)))
    for p in probs:
        p["it"]["st_scr"][p["it"]["base"] + p["h"]] = (
            p["st"] * p["el"] + jnp.dot(p["vh"].T.astype(BF16), p["kd"], preferred_element_type=F32))


def _gla_scan_kernel(qkf_ref, vf_ref, smf_ref, qkb_ref, vb_ref, smb_ref, upf_ref, upb_ref, bias_ref,
                     yf_ref, yb_ref, sf_scr, sb_scr, lg_scr):
    @pl.when(pl.program_id(1) == 0)
    def _():
        sf_scr[...] = jnp.zeros_like(sf_scr)
        sb_scr[...] = jnp.zeros_like(sb_scr)

    items = []
    for bi in range(qkf_ref.shape[0]):
        items.append(dict(qk=qkf_ref[bi], v=vf_ref[bi], sm=smf_ref[bi], up=upf_ref[...], bias=bias_ref[0:1, :],
                          causal=True, st_scr=sf_scr, base=bi * D_HEADS, y_ref=yf_ref, bi=bi,
                          qk_ref=qkf_ref, v_ref=vf_ref))
        items.append(dict(qk=qkb_ref[bi], v=vb_ref[bi], sm=smb_ref[bi], up=upb_ref[...], bias=bias_ref[1:2, :],
                          causal=False, st_scr=sb_scr, base=bi * D_HEADS, y_ref=yb_ref, bi=bi,
                          qk_ref=qkb_ref, v_ref=vb_ref))
    _gla_gates(items)
    lowest = items[0]["cum"]
    for it in items[1:]:
        lowest = jnp.minimum(lowest, it["cum"])
    in_range = jnp.min(lowest) > -GLA_SAFE
    pl.when(in_range)(lambda: _gla_solve(items))
    pl.when(jnp.logical_not(in_range))(lambda: _gla_tokenwise(items, lg_scr))


def _gla_scan(qk, v, small, gk_up_f, gk_b_f, gk_up_b, gk_b_b, n_ctx):
    bsz, n, _ = qk.shape
    L = D_CHUNK
    nc, ncc = n // L, n_ctx // L
    bw = _bwd_chunk(ncc, nc)
    hk = D_HEADS * D_KEY
    bias = jnp.zeros((8, hk), F32).at[0].set(gk_b_f).at[1].set(gk_b_b)
    nb = GLA_NB if bsz % GLA_NB == 0 else 1
    fx = lambda w: pl.BlockSpec((nb, L, w), lambda b, s: (b, s, 0))
    bx = lambda w: pl.BlockSpec((nb, L, w), lambda b, s: (b, bw(s), 0))
    return pl.pallas_call(
        _gla_scan_kernel,
        grid=(bsz // nb, nc),
        in_specs=[fx(2 * hk), fx(D_WIDTH), fx(128), bx(2 * hk), bx(D_WIDTH), bx(128),
                  _const_spec((D_LORA, hk)), _const_spec((D_LORA, hk)), _const_spec((8, hk))],
        out_specs=[fx(D_WIDTH), bx(D_WIDTH)],
        out_shape=[jax.ShapeDtypeStruct((bsz, n, D_WIDTH), F32)] * 2,
        scratch_shapes=[pltpu.VMEM((nb * D_HEADS, D_VAL, D_KEY), F32)] * 2 + [pltpu.VMEM((2 * nb, L, hk), F32)],
        compiler_params=_cp("parallel", "arbitrary"),
    )(qk, v, small, qk, v, small, gk_up_f, gk_up_b, bias)


def kernel(x, c, ctx, c_ctx, l0_mod_w, l0_mod_b, l0_norm1, l0_norm2, l0_w_in, l0_w_out, l0_mlp_w1, l0_mlp_w2, l0_lam_q1, l0_lam_k1, l0_lam_q2, l0_lam_k2, l0_subln, l0_mu, l0_w0_f, l0_w2_f, l0_w0_b, l0_w2_b, l0_a0, l0_a2, l0_g2, l0_k_k, l0_k_a, l0_r_k, l0_lnx_w, l0_lnx_b, l1_mod_w, l1_mod_b, l1_norm1, l1_norm2, l1_w_in, l1_w_out, l1_mlp_w1, l1_mlp_w2, l1_conv_w, l1_conv_b, l1_dt_bias_f, l1_a_log_f, l1_dt_bias_b, l1_a_log_b, l1_d_skip, l1_ssm_norm, l1_gk_up_f, l1_gk_b_f, l1_gk_up_b, l1_gk_b_b, l1_gla_norm, final_norm):
    bsz, n_lat, d = x.shape
    n_ctx = ctx.shape[1]
    assert d == D and n_ctx % TM == 0 and n_lat % TM == 0 and n_lat % GRID_W == 0
    nct = n_ctx // TM
    xc = jnp.concatenate([ctx, x], axis=1)

    mod0 = _modulation(c, c_ctx, l0_mod_w, l0_mod_b)
    qkv, pb = _inproj0(xc, mod0, l0_norm1, l0_w_in, _rope_tables(n_ctx, n_lat), nct)
    lamv = jnp.stack([l0_lam_q1, l0_lam_k1, l0_lam_q2, l0_lam_k2], axis=0)
    oa = _attention(qkv, lamv, l0_subln, 0, n_ctx)
    yf_b, yb_b, bonus, gate = _rwkv_mix(pb, n_ctx, l0_mu, l0_w0_f, l0_w2_f, l0_w0_b, l0_w2_b, l0_a0, l0_a2, l0_g2,
                                        l0_k_k, l0_k_a, l0_r_k)
    xc = _post(xc, _mix_even, [oa, yf_b, yb_b, bonus, gate],
               [l0_lnx_w.reshape(1, -1), l0_lnx_b.reshape(1, -1), _block_ones(B_WIDTH, B_HEAD)],
               mod0, l0_w_out, l0_norm2, l0_mlp_w1, l0_mlp_w2, None, nct, False)

    mod1 = _modulation(c, c_ctx, l1_mod_w, l1_mod_b)
    z, xact, qk, v, gg, small = _inproj1(xc, mod1, l1_norm1, l1_w_in, l1_conv_w, l1_conv_b, nct)
    yf_c, yb_c = _ssd_scan(xact, small, jnp.swapaxes(small, 1, 2), l1_dt_bias_f, l1_a_log_f,
                           l1_dt_bias_b, l1_a_log_b, n_ctx)
    yf_d, yb_d = _gla_scan(qk, v, small, l1_gk_up_f, l1_gk_b_f, l1_gk_up_b, l1_gk_b_b, n_ctx)
    return _post(xc, _mix_odd, [yf_c, yb_c, xact, z, yf_d, yb_d, gg],
                 [jnp.repeat(l1_d_skip, C_HEAD).reshape(1, C_WIDTH), l1_ssm_norm.reshape(1, C_WIDTH),
                  l1_gla_norm.reshape(1, D_VAL)],
                 mod1, l1_w_out, l1_norm2, l1_mlp_w1, l1_mlp_w2, final_norm, nct, True)
```

```python
import functools
import math

import jax
import jax.numpy as jnp
from jax import lax
from jax.experimental import pallas as pl
from jax.experimental.pallas import tpu as pltpu

F32 = jnp.float32
BF16 = jnp.bfloat16
HI = lax.Precision.HIGHEST

D = 1024
EPS = 1e-6
GRID_W = 64
ROPE_THETA = 10000.0
MLP_HIDDEN = 4 * D

A_HEADS, A_QK, A_V = 4, 64, 128
A_PROJ = 1536
A_SUBLN_EPS = 1e-5
B_HEADS, B_HEAD = 8, 64
B_WIDTH = 512
B_PROJ = 1792
B_GN_EPS = 64e-5
C_HEADS, C_HEAD, C_GROUPS, C_REP, C_STATE = 8, 64, 2, 4, 128
C_WIDTH = 512
C_CONV = 5
C_CHUNK = 128
C_CONV_DIM = 1024
D_HEADS, D_KEY, D_VAL = 4, 64, 128
D_WIDTH = 512
D_LORA = 16
D_TAU = 16.0
D_CHUNK = 64
ODD_PAD = 3200

TM = 256
TQ = 128
B_CHUNK = 64
B_GROUP = 4
ATTN_KC = 512
SSD_NB = 4
GLA_SAFE = 60.0
GLA_NB = 4
HALO = 8
VMEM_LIMIT = 56 * 1024 * 1024


def _cp(*sem):
    return pltpu.CompilerParams(dimension_semantics=sem, vmem_limit_bytes=VMEM_LIMIT)


def _const_spec(shape):
    nd = len(shape)
    return pl.BlockSpec(shape, lambda *_: (0,) * nd, pipeline_mode=pl.Buffered(1))


def _softplus(x):
    return jnp.maximum(x, 0.0) + jnp.log(1.0 + jnp.exp(-jnp.abs(x)))


def _sigmoid(x):
    return 1.0 / (1.0 + jnp.exp(-x))


def _silu(x):
    return x * _sigmoid(x)


def _ada(x, g, sh, sc):
    ms = jnp.mean(x * x, axis=-1, keepdims=True)
    return (x * lax.rsqrt(ms + EPS) * g) * (1.0 + sc) + sh


def _segsum(x, e, pieces=3):
    hi = x.astype(BF16)
    r1 = x - hi.astype(F32)
    mid = r1.astype(BF16)
    d = lambda a: jnp.dot(a, e, preferred_element_type=F32)
    if pieces == 2:
        return d(hi) + d(mid)
    lo = (r1 - mid.astype(F32)).astype(BF16)
    return d(hi) + d(mid) + d(lo)


def _block_ones(n, seg):
    i = jnp.arange(n) // seg
    return (i[:, None] == i[None, :]).astype(BF16)


def _tile_is_lat(nct):
    return lambda b, i: (b, (i >= nct).astype(jnp.int32), 0, 0)


def _mod_kernel(c_ref, w_ref, b_ref, o_ref):
    s = _silu(c_ref[...])
    o_ref[...] = jnp.dot(s, w_ref[...], precision=HI, preferred_element_type=F32) + b_ref[...]


def _modulation(c, c_ctx, mod_w, mod_b):
    bsz = c.shape[0]
    rows = 8 * ((bsz + 1 + 7) // 8)
    cs = jnp.zeros((rows, D), F32).at[0].set(c_ctx).at[1:bsz + 1].set(c)
    n6 = mod_w.shape[1]
    tn = 1024
    m = pl.pallas_call(
        _mod_kernel,
        grid=(n6 // tn,),
        in_specs=[pl.BlockSpec((rows, D), lambda j: (0, 0)),
                  pl.BlockSpec((D, tn), lambda j: (0, j)),
                  pl.BlockSpec((1, tn), lambda j: (0, j))],
        out_specs=pl.BlockSpec((rows, tn), lambda j: (0, j)),
        out_shape=jax.ShapeDtypeStruct((rows, n6), F32),
        compiler_params=_cp("arbitrary"),
    )(cs, mod_w, mod_b.reshape(1, n6))
    m_c = jnp.broadcast_to(m[0:1], (bsz, n6))
    return jnp.stack([m_c, m[1:bsz + 1]], axis=1)[:, :, None, :]


def _rope_tables(n_ctx, n_lat):
    t = jnp.arange(n_lat)
    row = (t // GRID_W).astype(F32)
    col = (t % GRID_W).astype(F32)
    nf = A_QK // 4
    inv = ROPE_THETA ** (-jnp.arange(nf, dtype=F32) / nf)
    ar = row[:, None] * inv[None, :]
    ac = col[:, None] * inv[None, :]
    z = jnp.zeros_like(ar)
    c64 = jnp.concatenate([jnp.cos(ar), jnp.cos(ar), jnp.cos(ac), jnp.cos(ac)], axis=-1)
    s1 = jnp.concatenate([-jnp.sin(ar), z, -jnp.sin(ac), z], axis=-1)
    s2 = jnp.concatenate([z, jnp.sin(ar), z, jnp.sin(ac)], axis=-1)
    pad = lambda a, v: jnp.concatenate([jnp.full((n_ctx, 128), v, F32), jnp.tile(a, (1, 2))], axis=0)
    return pad(c64, 1.0), pad(s1, 0.0), pad(s2, 0.0)


def _stream_specs(xs, nct, t0=0):
    if len(xs) == 1:
        return [pl.BlockSpec((None, TM, D), lambda b, i: (b, i + t0, 0))]
    return [pl.BlockSpec((None, TM, D), lambda b, i: (b, jnp.minimum(i + t0, nct - 1), 0)),
            pl.BlockSpec((None, TM, D), lambda b, i: (b, jnp.maximum(i + t0 - nct, 0), 0))]


def _stream_tile(x_refs, nct, t0=0):
    if len(x_refs) == 1:
        return x_refs[0][...]
    return jnp.where(pl.program_id(1) + t0 < nct, x_refs[0][...], x_refs[1][...])


def _inproj0_kernel(*refs, n_x, nct):
    x_refs = refs[:n_x]
    mod_ref, g_ref, wa_ref, wb_ref, c_ref, s1_ref, s2_ref, qkv_ref, pb_ref = refs[n_x:]
    mod = mod_ref[...]
    h = _ada(_stream_tile(x_refs, nct), g_ref[...], mod[:, 0:D], mod[:, D:2 * D]).astype(BF16)
    pa = jnp.dot(h, wa_ref[...], preferred_element_type=F32)
    c, s1, s2 = c_ref[...], s1_ref[...], s2_ref[...]
    scale = A_QK ** -0.5 * math.log2(math.e)
    for j in range(8):
        blk = pa[:, j * 128:(j + 1) * 128]
        rot = blk * c + pltpu.roll(blk, 112, 1) * s1 + pltpu.roll(blk, 16, 1) * s2
        if j < 4:
            rot = rot * scale
        qkv_ref[:, j * 128:(j + 1) * 128] = rot.astype(BF16)
    qkv_ref[:, 1024:1536] = pa[:, 1024:1536].astype(BF16)
    pb_ref[...] = jnp.dot(h, wb_ref[...], preferred_element_type=F32)


def _inproj0(xs, mod, norm1, w_in, tabs, nct):
    bsz = xs[0].shape[0]
    n = sum(a.shape[1] for a in xs)
    wa = w_in[:, :A_PROJ].astype(BF16)
    wb = w_in[:, A_PROJ:].astype(BF16)
    tok = lambda w: pl.BlockSpec((None, TM, w), lambda b, i: (b, i, 0))
    tab = pl.BlockSpec((TM, 128), lambda b, i: (i, 0))
    return pl.pallas_call(
        functools.partial(_inproj0_kernel, n_x=len(xs), nct=nct),
        grid=(bsz, n // TM),
        in_specs=_stream_specs(xs, nct) + [pl.BlockSpec((None, None, 1, 6 * D), _tile_is_lat(nct)),
                  _const_spec((1, D)), _const_spec((D, A_PROJ)), _const_spec((D, B_PROJ)),
                  tab, tab, tab],
        out_specs=[tok(A_PROJ), tok(B_PROJ)],
        out_shape=[jax.ShapeDtypeStruct((bsz, n, A_PROJ), BF16),
                   jax.ShapeDtypeStruct((bsz, n, B_PROJ), F32)],
        compiler_params=_cp("parallel", "parallel"),
    )(*xs, mod, norm1.reshape(1, D), wa, wb, *tabs)


def _attn_kernel(lam_ref, q_ref, k_ref, v_ref, subln_ref, o_ref, vt_scr, s0_scr, s1_scr, m0_scr, m1_scr, *,
                 lam_init, n_ctx, nq_ctx, nq):
    qi = pl.program_id(2)
    n_all = k_ref.shape[0]

    @pl.when(qi == 0)
    def _():
        step = ATTN_KC
        for c0 in range(0, n_all, step):
            c1 = min(c0 + step, n_all)
            vt_scr[:, c0:c1] = v_ref[c0:c1, :].astype(F32).T.astype(BF16)

    lv = lam_ref[...]
    lam = (jnp.exp(jnp.sum(lv[0:1] * lv[1:2], axis=-1, keepdims=True))
           - jnp.exp(jnp.sum(lv[2:3] * lv[3:4], axis=-1, keepdims=True)) + lam_init)
    ts = q_ref.shape[0]

    def tree(vals, op):
        while len(vals) > 1:
            vals = [op(vals[i], vals[i + 1]) if i + 1 < len(vals) else vals[i] for i in range(0, len(vals), 2)]
        return vals[0]

    def stage(chunks_cur, chunks_prev, parity):
        s_cur, s_prev = (s0_scr, s1_scr) if parity == 0 else (s1_scr, s0_scr)
        m_cur, m_prev_scr = (m0_scr, m1_scr) if parity == 0 else (m1_scr, m0_scr)
        if chunks_cur:
            q = q_ref[...]
            lane = lax.broadcasted_iota(jnp.int32, q.shape, 1)
            zero = jnp.zeros_like(q)
            q2 = jnp.concatenate([jnp.where(lane < A_QK, q, zero), jnp.where(lane >= A_QK, q, zero)], axis=0)
        if chunks_prev:
            m_prev = m_prev_scr[0:1, :]
        ms, ls, accs = [], [], []
        for ci in range(max(len(chunks_cur), len(chunks_prev))):
            if ci < len(chunks_cur):
                c0, cn = chunks_cur[ci]
                s = _nt_dot(k_ref[c0:c0 + cn, :], q2)
                s_cur[c0:c0 + cn, :] = s
                ms.append(jnp.max(s, axis=0, keepdims=True))
            if ci < len(chunks_prev):
                c0, cn = chunks_prev[ci]
                p = jnp.exp2(s_prev[c0:c0 + cn, :] - m_prev)
                ls.append(jnp.sum(p, axis=0, keepdims=True))
                accs.append(jnp.dot(vt_scr[:, c0:c0 + cn], p.astype(BF16), preferred_element_type=F32))
        if chunks_cur:
            m_cur[...] = jnp.broadcast_to(tree(ms, jnp.maximum), m_cur.shape)
        if chunks_prev:
            on = tree(accs, jnp.add) / tree(ls, jnp.add)
            o = (on[:, :ts] - lam * on[:, ts:]).T
            msq = jnp.mean(o * o, axis=-1, keepdims=True)
            o_ref[...] = (o * lax.rsqrt(msq + A_SUBLN_EPS) * subln_ref[...]) * (1.0 - lam_init)
        else:
            o_ref[...] = jnp.zeros_like(o_ref)

    kc = min(ATTN_KC, n_all - n_ctx)
    ctx_chunks = [(0, n_ctx)]
    all_chunks = ctx_chunks + [(n_ctx + j * kc, kc) for j in range((n_all - n_ctx) // kc)]
    for parity in (0, 1):
        on = lambda cond: pl.when(jnp.logical_and(cond, qi % 2 == parity))
        if parity == 0:
            on(qi == 0)(functools.partial(stage, ctx_chunks, [], parity))
        if nq_ctx > 1:
            on(jnp.logical_and(qi >= 1, qi < nq_ctx))(functools.partial(stage, ctx_chunks, ctx_chunks, parity))
        if nq_ctx % 2 == parity:
            on(qi == nq_ctx)(functools.partial(stage, all_chunks, ctx_chunks, parity))
        on(jnp.logical_and(qi > nq_ctx, qi < nq))(functools.partial(stage, all_chunks, all_chunks, parity))
        if nq % 2 == parity:
            on(qi == nq)(functools.partial(stage, [], all_chunks, parity))


def _attention(qkv, lamv, subln, layer_idx, n_ctx):
    bsz, n, _ = qkv.shape
    lam_init = 0.8 - 0.6 * math.exp(-0.3 * layer_idx)
    nq = n // TQ
    assert n_ctx % TQ == 0 and n_ctx < n
    kern = functools.partial(_attn_kernel, lam_init=lam_init, n_ctx=n_ctx, nq_ctx=n_ctx // TQ, nq=nq)
    return pl.pallas_call(
        kern,
        grid=(bsz, A_HEADS, nq + 1),
        in_specs=[_const_spec((4, A_QK)),
                  pl.BlockSpec((None, TQ, 128), lambda b, h, i: (b, jnp.minimum(i, nq - 1), h)),
                  pl.BlockSpec((None, n, 128), lambda b, h, i: (b, 0, A_HEADS + h)),
                  pl.BlockSpec((None, n, 128), lambda b, h, i: (b, 0, 2 * A_HEADS + h)),
                  _const_spec((1, A_V))],
        out_specs=pl.BlockSpec((None, TQ, 128), lambda b, h, i: (b, jnp.maximum(i - 1, 0), h)),
        out_shape=jax.ShapeDtypeStruct((bsz, n, A_HEADS * A_V), F32),
        scratch_shapes=[pltpu.VMEM((A_V, n), BF16), pltpu.VMEM((n, 2 * TQ), F32), pltpu.VMEM((n, 2 * TQ), F32),
                        pltpu.VMEM((8, 2 * TQ), F32), pltpu.VMEM((8, 2 * TQ), F32)],
        compiler_params=_cp("parallel", "parallel", "arbitrary"),
    )(lamv, qkv, qkv, qkv, subln.reshape(1, A_V))


def _halo_flags(i, nct, nt):
    prev_ok = jnp.logical_and(i != 0, i != nct)
    next_ok = jnp.logical_and(i != nct - 1, i != nt - 1)
    return prev_ok, next_ok


def _rwkv_prep_kernel(p_ref, pp_ref, pn_ref, mu_ref, vec_ref, w2f_ref, w2b_ref, a2_ref, g2_ref, e_ref,
                      r_ref, wf_ref, wb_ref, k_ref, v_ref, kk_ref, kb_ref, bonus_ref, g_ref, *, nct, nt):
    i = pl.program_id(1)
    f = p_ref[...]
    tm = f.shape[0]
    prev_ok, next_ok = _halo_flags(i, nct, nt)
    hp = jnp.where(prev_ok, pp_ref[HALO - 1:HALO, :], 0.0)
    hn = jnp.where(next_ok, pn_ref[0:1, :], 0.0)
    row = lax.broadcasted_iota(jnp.int32, f.shape, 0)
    prev = jnp.where(row == 0, hp, pltpu.roll(f, 1, 0))
    nxt = jnp.where(row == tm - 1, hn, pltpu.roll(f, tm - 1, 0))
    fm = f + (0.5 * (prev + nxt) - f) * mu_ref[...]
    r = fm[:, 0:512]
    k = fm[:, 512:1024]
    v = fm[:, 1024:1536]
    wd = fm[:, 1536:1600]
    ad = fm[:, 1600:1664]
    gd = fm[:, 1664:1792]
    w0f, w0b, a0 = vec_ref[0:1, :], vec_ref[1:2, :], vec_ref[2:3, :]
    k_k, k_a, r_k = vec_ref[3:4, :], vec_ref[4:5, :], vec_ref[5:6, :]
    e = e_ref[...]
    mm = lambda a, w_ref: jnp.dot(a.astype(BF16), w_ref[...], preferred_element_type=F32)
    a = _sigmoid(a0 + mm(ad, a2_ref))
    tw = jnp.tanh(wd)
    decay = lambda w0, w2_ref: -jnp.exp(-_softplus(-(w0 + mm(tw, w2_ref))) - 0.5)
    kkr = k * k_k
    kk = kkr / jnp.maximum(jnp.sqrt(_segsum(kkr * kkr, e, 2)), 1e-12)
    k2 = k * (1.0 + (a - 1.0) * k_a)
    r_ref[...] = r
    wf_ref[...] = decay(w0f, w2f_ref)
    wb_ref[...] = decay(w0b, w2b_ref)
    k_ref[...] = k2
    v_ref[...] = v.astype(BF16)
    kk_ref[...] = kk
    kb_ref[...] = kk * a
    bonus_ref[...] = (_segsum(r * k2 * r_k, e, 2) * v).astype(BF16)
    g_ref[...] = mm(_sigmoid(gd), g2_ref).astype(BF16)


def _rwkv_prep(pb, mu, vecs, w2_f, w2_b, a2, g2, nct):
    bsz, n, _ = pb.shape
    nt = n // TM
    hb = TM // HALO
    tok = lambda w: pl.BlockSpec((None, TM, w), lambda b, i: (b, i, 0))
    kern = functools.partial(_rwkv_prep_kernel, nct=nct, nt=nt)
    outs = pl.pallas_call(
        kern,
        grid=(bsz, nt),
        in_specs=[tok(B_PROJ),
                  pl.BlockSpec((None, HALO, B_PROJ), lambda b, i: (b, jnp.maximum(i * hb - 1, 0), 0)),
                  pl.BlockSpec((None, HALO, B_PROJ), lambda b, i: (b, jnp.minimum((i + 1) * hb, nt * hb - 1), 0)),
                  _const_spec((1, B_PROJ)), _const_spec((8, B_WIDTH)),
                  _const_spec((64, B_WIDTH)), _const_spec((64, B_WIDTH)), _const_spec((64, B_WIDTH)),
                  _const_spec((128, B_WIDTH)), _const_spec((B_WIDTH, B_WIDTH))],
        out_specs=[tok(B_WIDTH)] * 9,
        out_shape=[jax.ShapeDtypeStruct((bsz, n, B_WIDTH), dt)
                   for dt in (F32, F32, F32, F32, BF16, F32, F32, BF16, BF16)],
        compiler_params=_cp("parallel", "parallel"),
    )(pb, pb, pb, mu.reshape(1, B_PROJ), vecs, w2_f.astype(BF16), w2_b.astype(BF16), a2.astype(BF16),
      g2.astype(BF16), _block_ones(B_WIDTH, B_HEAD))
    return outs


def _cumsum3(tri_b, x):
    hi = x.astype(BF16)
    r1 = x - hi.astype(F32)
    mid = r1.astype(BF16)
    lo = (r1 - mid.astype(F32)).astype(BF16)
    d = lambda a: jnp.dot(tri_b, a, preferred_element_type=F32)
    return d(hi) + d(mid) + d(lo)


def _head_spread(heads, width):
    row = lax.broadcasted_iota(jnp.int32, (heads, heads * width), 0)
    col = lax.broadcasted_iota(jnp.int32, (heads, heads * width), 1)
    return (row == col // width).astype(BF16)


def _bdot(a, b):
    return jnp.dot(a.astype(BF16), b.astype(BF16), preferred_element_type=F32)


def _rwkv_dir(r, lw, k, v, kk, kb, s_scr, y_ref, causal):
    length = r.shape[0]
    assert length == B_HEAD
    gw = B_GROUP * B_HEAD
    ii = lax.broadcasted_iota(jnp.int32, (length, gw), 0)
    jj = lax.broadcasted_iota(jnp.int32, (length, gw), 1) % length
    tri = (ii >= jj) if causal else (ii <= jj)
    strict = jnp.logical_and(tri, ii != jj)
    lvl = []
    m = 1
    while m < length:
        same2 = (ii // (2 * m)) == (jj // (2 * m))
        diff1 = (ii // m) != (jj // m)
        lvl.append(jnp.logical_and(jnp.logical_and(same2, diff1), strict))
        m *= 2
    tri_sq, _ = _scan_masks(length, causal)
    cum = _cumsum3(tri_sq.astype(BF16), lw)
    last = length - 1 if causal else 0
    cl = cum[last:last + 1, :]
    gi = jnp.exp(-cum)
    gl = jnp.exp(cl - cum)
    kg = kk * jnp.exp(cum - lw)
    rg = r * jnp.exp(cum)
    ki, bi = k * gi, kb * gi
    kd, bd = k * gl, kb * gl
    el = jnp.exp(cl)
    probs = []
    for g in range(r.shape[1] // gw):
        sl = slice(g * gw, (g + 1) * gw)
        probs.append(dict(
            sl=sl, tri=tri, strict=strict, lvl=lvl, s_scr=s_scr, y_ref=y_ref, g=g,
            lhs=jnp.concatenate([kg[:, sl], rg[:, sl]], axis=0).astype(BF16),
            bi=bi[:, sl], ki=ki[:, sl], v=v[:, sl],
            kbd=jnp.concatenate([kd[:, sl], -bd[:, sl]], axis=0).astype(BF16),
            el=el[:, sl]))
    return probs


def _rwkv_solve(probs):
    length = probs[0]["v"].shape[0]
    gw = probs[0]["v"].shape[1]
    hrow = lax.broadcasted_iota(jnp.int32, (gw, gw), 0) // B_HEAD
    hcol = lax.broadcasted_iota(jnp.int32, (gw, gw), 1) // B_HEAD
    same_head = hrow == hcol
    same_head_b = same_head.astype(BF16)

    def bd(x):
        xb = x.astype(BF16)
        return jnp.concatenate([xb] * B_GROUP, axis=0) * same_head_b

    mm = lambda a, w: jnp.dot(a.astype(BF16), w, preferred_element_type=F32)
    for p in probs:
        p["sc_b"] = _nt_dot(p["lhs"], bd(p["bi"]))
        p["sc_k"] = _nt_dot(p["lhs"], bd(p["ki"]))
    for p in probs:
        p["a_kb"] = jnp.where(p["strict"], p["sc_b"][:length], 0.0)
        p["a_kk"] = jnp.where(p["strict"], p["sc_k"][:length], 0.0).astype(BF16)
        p["a_rb"] = jnp.where(p["tri"], p["sc_b"][length:], 0.0).astype(BF16)
        p["a_rk"] = jnp.where(p["tri"], p["sc_k"][length:], 0.0).astype(BF16)
        p["nn"] = -jnp.where(p["lvl"][0], p["a_kb"], 0.0)
        p["s0"] = p["s_scr"][p["g"]]
        p["vbd"] = bd(p["v"])
    for p in probs:
        p["hs"] = _nt_dot(p["lhs"], bd(p["s0"]))
        p["akv"] = mm(p["a_kk"], p["vbd"])
    for lv in range(1, len(probs[0]["lvl"])):
        for p in probs:
            aoff = jnp.where(p["lvl"][lv], p["a_kb"], 0.0)
            p["x"] = aoff + mm(p["nn"], bd(aoff))
        for p in probs:
            p["nn"] = p["nn"] - (p["x"] + mm(p["x"], bd(p["nn"])))
    for p in probs:
        rhs = p["hs"][:length] + p["akv"]
        p["u"] = rhs + mm(p["nn"], bd(rhs))
    for p in probs:
        p["y_ref"][:, p["sl"]] = p["hs"][length:] + mm(p["a_rk"], p["vbd"]) - mm(p["a_rb"], bd(p["u"]))
    for p in probs:
        vu = jnp.concatenate([p["v"].astype(F32), p["u"]], axis=0)
        z = jnp.dot(vu.T.astype(BF16), p["kbd"], preferred_element_type=F32)
        z = jnp.where(same_head, z, 0.0)
        fold = z[0:B_HEAD]
        for j in range(1, B_GROUP):
            fold = fold + z[j * B_HEAD:(j + 1) * B_HEAD]
        p["s_scr"][p["g"]] = p["s0"] * p["el"] + fold


def _rwkv_chunk_kernel(*refs):
    fwd, bwd = refs[0:6], refs[6:12]
    yf_ref, yb_ref, sf_scr, sb_scr = refs[12:16]

    @pl.when(pl.program_id(1) == 0)
    def _():
        sf_scr[...] = jnp.zeros_like(sf_scr)
        sb_scr[...] = jnp.zeros_like(sb_scr)

    probs = _rwkv_dir(*[x[...] for x in fwd], sf_scr, yf_ref, True)
    probs += _rwkv_dir(*[x[...] for x in bwd], sb_scr, yb_ref, False)
    _rwkv_solve(probs)


def _rwkv_chunk_scan(r, lwf, lwb, k, v, kk, kb, n_ctx):
    bsz, n, width = r.shape
    L = B_CHUNK
    nc, ncc = n // L, n_ctx // L
    bw = _bwd_chunk(ncc, nc)
    fx = pl.BlockSpec((None, L, width), lambda b, s: (b, s, 0))
    bx = pl.BlockSpec((None, L, width), lambda b, s: (b, bw(s), 0))
    return pl.pallas_call(
        _rwkv_chunk_kernel,
        grid=(bsz, nc),
        in_specs=[fx] * 6 + [bx] * 6,
        out_specs=[fx, bx],
        out_shape=[jax.ShapeDtypeStruct((bsz, n, width), F32)] * 2,
        scratch_shapes=[pltpu.VMEM((B_HEADS // B_GROUP, B_HEAD, B_GROUP * B_HEAD), F32)] * 2,
        compiler_params=_cp("parallel", "arbitrary"),
    )(r, lwf, k, v, kk, kb, r, lwb, k, v, kk, kb)


def _rwkv_mix(pb, n_ctx, mu, w0_f, w2_f, w0_b, w2_b, a0, a2, g2, k_k, k_a, r_k):
    zero = jnp.zeros_like(w0_f)
    vecs = jnp.stack([w0_f, w0_b, a0, k_k, k_a, r_k, zero, zero], axis=0)
    r, lwf, lwb, k2, v, kk, kb, bonus, g = _rwkv_prep(pb, mu, vecs, w2_f, w2_b, a2, g2, n_ctx // TM)
    yf, yb = _rwkv_chunk_scan(r, lwf, lwb, k2, v, kk, kb, n_ctx)
    return yf, yb, bonus, g


def _mix_even(tok, cst):
    oa, yf, yb, bonus, g = tok
    lnw, lnb, e = cst
    y = yf + yb
    d = y - _segsum(y, e, 2) * (1.0 / B_HEAD)
    var = _segsum(d * d, e, 2) * (1.0 / B_HEAD)
    return oa,(d * lax.rsqrt(var + B_GN_EPS) * lnw + lnb + bonus) * g


def _mix_odd(tok, cst):
    yfc, ybc, xs, z, yfd, ybd, gg = tok
    dsk, ngc, ngd = cst
    y = (yfc + ybc + xs * dsk) * _silu(z)
    gw = C_WIDTH // C_GROUPS
    oc = []
    for g in range(C_GROUPS):
        blk = y[:, g * gw:(g + 1) * gw]
        ms = jnp.mean(blk * blk, axis=-1, keepdims=True)
        oc.append(blk * lax.rsqrt(ms + EPS) * ngc[:, g * gw:(g + 1) * gw])
    y = yfd + ybd
    gate = _silu(gg)
    od = []
    for h in range(D_HEADS):
        sl = slice(h * D_VAL, (h + 1) * D_VAL)
        blk = y[:, sl]
        ms = jnp.mean(blk * blk, axis=-1, keepdims=True)
        od.append(blk * lax.rsqrt(ms + EPS) * ngd * gate[:, sl])
    return jnp.concatenate(oc, axis=-1), jnp.concatenate(od, axis=-1)


def _post_kernel(*refs, mix_fn, n_x, nct, t0, n_tok, n_cst, final):
    x_refs, refs = refs[:n_x], refs[n_x:]
    tok = [r[...] for r in refs[:n_tok]]
    cst = [r[...] for r in refs[n_tok:n_tok + n_cst]]
    mod_ref, woa_ref, wob_ref, n2_ref, w1_ref, w2_ref, fn_ref, out_ref = refs[n_tok + n_cst:]
    mod = mod_ref[...]
    g1, sh2 = mod[:, 2 * D:3 * D], mod[:, 3 * D:4 * D]
    sc2, g2 = mod[:, 4 * D:5 * D], mod[:, 5 * D:6 * D]
    o1, o2 = mix_fn(tok, cst)
    mix = (jnp.dot(o1.astype(BF16), woa_ref[...], preferred_element_type=F32)
           + jnp.dot(o2.astype(BF16), wob_ref[...], preferred_element_type=F32))
    x1 = _stream_tile(x_refs, nct, t0) + g1 * mix
    h = _ada(x1, n2_ref[...], sh2, sc2).astype(BF16)
    hid = jnp.maximum(jnp.dot(h, w1_ref[...], preferred_element_type=F32), 0.0)
    hid = (hid * hid).astype(BF16)
    x2 = x1 + g2 * jnp.dot(hid, w2_ref[...], preferred_element_type=F32)
    if final:
        ms = jnp.mean(x2 * x2, axis=-1, keepdims=True)
        x2 = x2 * lax.rsqrt(ms + EPS) * fn_ref[...]
    out_ref[...] = x2


def _post(xs, mix_fn, tok_in, cst_in, mod, w_out, norm2, mlp_w1, mlp_w2, final_norm, nct, lat_only):
    bsz = xs[0].shape[0]
    n = sum(a.shape[1] for a in xs)
    half = w_out.shape[0] // 2
    t0 = nct if lat_only else 0
    nt = n // TM - t0
    tok = lambda w: pl.BlockSpec((None, TM, w), lambda b, i: (b, i + t0, 0))
    final = final_norm is not None
    fn = (final_norm if final else jnp.ones((D,), F32)).reshape(1, D)
    kern = functools.partial(_post_kernel, mix_fn=mix_fn, n_x=len(xs), nct=nct, t0=t0, n_tok=len(tok_in),
                             n_cst=len(cst_in), final=final)
    return pl.pallas_call(
        kern,
        grid=(bsz, nt),
        in_specs=_stream_specs(xs, nct, t0) + [tok(half)] * len(tok_in) + [_const_spec(a.shape) for a in cst_in]
                 + [pl.BlockSpec((None, None, 1, 6 * D), lambda b, i: (b, (i + t0 >= nct).astype(jnp.int32), 0, 0)),
                    _const_spec((half, D)), _const_spec((half, D)), _const_spec((1, D)),
                    _const_spec((D, MLP_HIDDEN)), _const_spec((MLP_HIDDEN, D)), _const_spec((1, D))],
        out_specs=pl.BlockSpec((None, TM, D), lambda b, i: (b, i, 0)),
        out_shape=jax.ShapeDtypeStruct((bsz, nt * TM, D), F32),
        compiler_params=_cp("parallel", "parallel"),
    )(*xs, *tok_in, *cst_in, mod, w_out[:half].astype(BF16), w_out[half:].astype(BF16), norm2.reshape(1, D),
      mlp_w1.astype(BF16), mlp_w2.astype(BF16), fn)


def _inproj1_kernel(x_ref, xp_ref, xn_ref, mod_ref, g_ref, w_ref, cw_ref, cb_ref,
                    z_ref, xact_ref, qk_ref, v_ref, gg_ref, sm_ref, *, nct, nt):
    mod = mod_ref[...]
    tm = x_ref.shape[0]
    ne = tm + 2 * HALO
    x_ext = jnp.concatenate([xp_ref[...], x_ref[...], xn_ref[...]], axis=0)
    h = _ada(x_ext, g_ref[...], mod[:, 0:D], mod[:, D:2 * D]).astype(BF16)
    xbc_ext = jnp.dot(h, w_ref[:, 512:1536], preferred_element_type=F32)
    h_main = h[HALO:HALO + tm]
    for ref, c0, c1 in ((z_ref, 0, 512), (qk_ref, 1536, 2048), (v_ref, 2048, 2560), (gg_ref, 2560, 3072),
                        (sm_ref, 3072, 3200)):
        ref[...] = jnp.dot(h_main, w_ref[:, c0:c1], preferred_element_type=F32)
    xbc = xbc_ext[HALO:HALO + tm]
    prev_ok, next_ok = _halo_flags(pl.program_id(1), nct, nt)
    row = lax.broadcasted_iota(jnp.int32, (ne, C_CONV_DIM), 0)
    dead = jnp.logical_or(jnp.logical_and(row < HALO, jnp.logical_not(prev_ok)),
                          jnp.logical_and(row >= HALO + tm, jnp.logical_not(next_ok)))
    ext = jnp.where(dead, 0.0, xbc_ext)
    acc = cb_ref[...] + cw_ref[2:3, :] * xbc
    for j in (0, 1, 3, 4):
        shifted = pltpu.roll(ext, (C_CONV // 2 - j) % ne, 0)[HALO:HALO + tm]
        acc = acc + cw_ref[j:j + 1, :] * shifted
    xact_ref[...] = _silu(acc)


def _halo_specs(width, nt):
    hb = TM // HALO
    return [pl.BlockSpec((None, HALO, width), lambda b, i: (b, jnp.maximum(i * hb - 1, 0), 0)),
            pl.BlockSpec((None, HALO, width), lambda b, i: (b, jnp.minimum((i + 1) * hb, nt * hb - 1), 0))]


def _inproj1(xc, mod, norm1, w_in, conv_w, conv_b, nct):
    bsz, n, _ = xc.shape
    nt = n // TM
    w8 = jnp.zeros((8, C_CONV_DIM), F32).at[:C_CONV].set(conv_w)
    c0 = C_WIDTH + C_CONV_DIM
    d0 = c0 + 2 * C_HEADS
    qk_w = 2 * D_HEADS * D_KEY
    w = jnp.concatenate([w_in[:, :c0], w_in[:, d0:d0 + qk_w + 2 * D_WIDTH], w_in[:, c0:d0],
                         w_in[:, d0 + qk_w + 2 * D_WIDTH:],
                         jnp.zeros((D, 128 - 2 * C_HEADS - D_LORA), F32)], axis=1).astype(BF16)
    tok = lambda wd: pl.BlockSpec((None, TM, wd), lambda b, i: (b, i, 0))
    widths = [512, 1024, 512, 512, 512, 128]
    return pl.pallas_call(
        functools.partial(_inproj1_kernel, nct=nct, nt=nt),
        grid=(bsz, nt),
        in_specs=[tok(D)] + _halo_specs(D, nt)
                 + [pl.BlockSpec((None, None, 1, 6 * D), _tile_is_lat(nct)),
                    _const_spec((1, D)), _const_spec((D, ODD_PAD)),
                    _const_spec((8, C_CONV_DIM)), _const_spec((1, C_CONV_DIM))],
        out_specs=[tok(wd) for wd in widths],
        out_shape=[jax.ShapeDtypeStruct((bsz, n, wd), F32) for wd in widths],
        compiler_params=_cp("parallel", "parallel"),
    )(xc, xc, xc, mod, norm1.reshape(1, D), w, w8, conv_b.reshape(1, C_CONV_DIM))


def _scan_masks(length, causal):
    ii = lax.broadcasted_iota(jnp.int32, (length, length), 0)
    jj = lax.broadcasted_iota(jnp.int32, (length, length), 1)
    tri = (ii >= jj) if causal else (ii <= jj)
    tri_t = (ii <= jj) if causal else (ii >= jj)
    return tri, tri_t


def _nt_dot(a, b):
    return lax.dot_general(a, b, (((1,), (1,)), ((), ())), preferred_element_type=F32)


def _ssd_dir(xa, sm, sm_t, col0, prm, prm_t, h_scr, y_ref, causal, bi):
    length = xa.shape[0]
    dtb, alog = prm[0:1, col0:col0 + C_HEADS], prm[1:2, col0:col0 + C_HEADS]
    dtb_t, alog_t = prm_t[col0:col0 + C_HEADS, 0:1], prm_t[col0:col0 + C_HEADS, 1:2]
    dt = _softplus(sm[:, col0:col0 + C_HEADS] + dtb)
    la = dt * (-jnp.exp(alog))
    la_t = _softplus(sm_t[col0:col0 + C_HEADS, :] + dtb_t) * (-jnp.exp(alog_t))
    tri, tri_t = _scan_masks(length, causal)
    cum = jnp.dot(tri.astype(F32), la, precision=HI, preferred_element_type=F32)
    cum_t = jnp.dot(la_t, tri_t.astype(F32), precision=HI, preferred_element_type=F32)
    last = length - 1 if causal else 0
    spread = lambda a, w: _segsum(a, _head_spread(C_HEADS, w))
    xdt_all = xa[:, :C_WIDTH] * spread(dt, C_HEAD)
    xw_all = xdt_all * spread(jnp.exp(cum[last:last + 1, :] - cum), C_HEAD)
    ech_all = spread(jnp.exp(cum), C_HEAD)
    cum_all = spread(cum, length)
    ecl_t = jnp.exp(cum_t[:, last:last + 1])
    gw = C_REP * C_HEAD
    groups, probs = [], []
    for g in range(C_GROUPS):
        bg = xa[:, C_WIDTH + g * C_STATE:C_WIDTH + (g + 1) * C_STATE].astype(BF16)
        cg = xa[:, C_WIDTH + (C_GROUPS + g) * C_STATE:C_WIDTH + (C_GROUPS + g + 1) * C_STATE].astype(BF16)
        ecl_rows = jnp.concatenate([jnp.broadcast_to(ecl_t[g * C_REP + rr:g * C_REP + rr + 1, :], (C_HEAD, 1))
                                    for rr in range(C_REP)], axis=0)
        grp = dict(bg=bg, cg=cg, gi=bi * C_GROUPS + g, h_scr=h_scr, ecl_rows=ecl_rows,
                   xw=xw_all[:, g * gw:(g + 1) * gw], ech=ech_all[:, g * gw:(g + 1) * gw])
        groups.append(grp)
        for rr in range(C_REP):
            h = g * C_REP + rr
            probs.append(dict(
                grp=grp, tri=tri, diff=cum_all[:, h * length:(h + 1) * length] - cum_t[h:h + 1, :],
                xdt=xdt_all[:, h * C_HEAD:(h + 1) * C_HEAD].astype(BF16)))
    return groups, probs, (y_ref, bi)


def _ssd_solve(dirs):
    groups = [g for gs, _, _ in dirs for g in gs]
    probs = [p for _, ps, _ in dirs for p in ps]
    for grp in groups:
        grp["cb"] = _nt_dot(grp["cg"], grp["bg"])
        grp["hs"] = grp["h_scr"][grp["gi"]]
    for p in probs:
        lmat = jnp.exp(jnp.where(p["tri"], p["diff"], -jnp.inf))
        p["m"] = (p["grp"]["cb"] * lmat).astype(BF16)
    for grp in groups:
        grp["y_inter"] = _nt_dot(grp["cg"], grp["hs"].astype(BF16)) * grp["ech"]
    for p in probs:
        p["y"] = jnp.dot(p["m"], p["xdt"], preferred_element_type=F32)
    for grp in groups:
        xw_t = grp["xw"].T.astype(BF16)
        grp["h_scr"][grp["gi"]] = (grp["hs"] * grp["ecl_rows"]
                                   + jnp.dot(xw_t, grp["bg"], preferred_element_type=F32))
    for gs, ps, (y_ref, bi) in dirs:
        y_ref[bi] = (jnp.concatenate([p["y"] for p in ps], axis=-1)
                     + jnp.concatenate([g["y_inter"] for g in gs], axis=-1))


def _ssd_scan_kernel(xf_ref, smf_ref, smtf_ref, xb_ref, smb_ref, smtb_ref, prm_ref, prmt_ref,
                     yf_ref, yb_ref, hf_scr, hb_scr):
    @pl.when(pl.program_id(1) == 0)
    def _():
        hf_scr[...] = jnp.zeros_like(hf_scr)
        hb_scr[...] = jnp.zeros_like(hb_scr)

    prm, prm_t = prm_ref[...], prmt_ref[...]
    dirs = []
    for bi in range(xf_ref.shape[0]):
        dirs.append(_ssd_dir(xf_ref[bi], smf_ref[bi], smtf_ref[bi], 0, prm, prm_t, hf_scr, yf_ref, True, bi))
        dirs.append(_ssd_dir(xb_ref[bi], smb_ref[bi], smtb_ref[bi], C_HEADS, prm, prm_t, hb_scr, yb_ref, False, bi))
    _ssd_solve(dirs)


def _bwd_chunk(ncc, nc):
    return lambda s: jnp.where(s < ncc, ncc - 1 - s, nc - 1 - s + ncc)


def _ssd_scan(xact, small, small_t, dt_bias_f, a_log_f, dt_bias_b, a_log_b, n_ctx):
    bsz, n, cdim = xact.shape
    L = C_CHUNK
    nc, ncc = n // L, n_ctx // L
    bw = _bwd_chunk(ncc, nc)
    prm = jnp.zeros((8, 128), F32).at[0, :16].set(jnp.concatenate([dt_bias_f, dt_bias_b]))
    prm = prm.at[1, :16].set(jnp.concatenate([a_log_f, a_log_b]))
    nb = SSD_NB if bsz % SSD_NB == 0 else 1
    fx = lambda w: pl.BlockSpec((nb, L, w), lambda b, s: (b, s, 0))
    bx = lambda w: pl.BlockSpec((nb, L, w), lambda b, s: (b, bw(s), 0))
    ft = pl.BlockSpec((nb, 128, L), lambda b, s: (b, 0, s))
    bt = pl.BlockSpec((nb, 128, L), lambda b, s: (b, 0, bw(s)))
    return pl.pallas_call(
        _ssd_scan_kernel,
        grid=(bsz // nb, nc),
        in_specs=[fx(cdim), fx(128), ft, bx(cdim), bx(128), bt, _const_spec((8, 128)), _const_spec((128, 8))],
        out_specs=[fx(C_WIDTH), bx(C_WIDTH)],
        out_shape=[jax.ShapeDtypeStruct((bsz, n, C_WIDTH), F32)] * 2,
        scratch_shapes=[pltpu.VMEM((nb * C_GROUPS, C_REP * C_HEAD, C_STATE), F32)] * 2,
        compiler_params=_cp("parallel", "arbitrary"),
    )(xact, small, small_t, xact, small, small_t, prm, prm.T)


def _gla_gates(items):
    for it in items:
        gd = it["sm"][:, 2 * C_HEADS:2 * C_HEADS + D_LORA]
        it["zz"] = jnp.dot(gd, it["up"], precision=HI, preferred_element_type=F32) + it["bias"]
    for it in items:
        it["lg"] = -_softplus(-it["zz"]) * (1.0 / D_TAU)
        it["tri"], _ = _scan_masks(it["lg"].shape[0], it["causal"])
        it["cum"] = _cumsum3(it["tri"].astype(BF16), it["lg"])


def _gla_tokenwise(items, lg_scr):
    length = lg_scr.shape[1]
    hk = D_HEADS * D_KEY
    eye = (lax.broadcasted_iota(jnp.int32, (D_VAL, D_VAL), 0)
           == lax.broadcasted_iota(jnp.int32, (D_VAL, D_VAL), 1)).astype(F32)
    for ii, it in enumerate(items):
        lg_scr[ii] = it["lg"]

    rows = 8

    def body(step, carry):
        for ii, it in enumerate(items):
            grp = step if it["causal"] else length // rows - 1 - step
            r0 = pl.multiple_of(grp * rows, rows)
            g_blk = jnp.exp(lg_scr[ii, pl.ds(r0, rows), :])
            qk_blk = it["qk_ref"][it["bi"], pl.ds(r0, rows), :]
            v_blk = it["v_ref"][it["bi"], pl.ds(r0, rows), :]
            for h in range(D_HEADS):
                sl = slice(h * D_KEY, (h + 1) * D_KEY)
                kl = slice(hk + h * D_KEY, hk + (h + 1) * D_KEY)
                vl = slice(h * D_VAL, (h + 1) * D_VAL)
                st = it["st_scr"][it["base"] + h]
                ys = [None] * rows
                for r in (range(rows) if it["causal"] else range(rows - 1, -1, -1)):
                    v_col = jnp.sum(eye * v_blk[r:r + 1, vl], axis=1, keepdims=True)
                    st = st * g_blk[r:r + 1, sl] + v_col * qk_blk[r:r + 1, kl]
                    y_col = jnp.sum(st * (qk_blk[r:r + 1, sl] * (D_KEY ** -0.5)), axis=1, keepdims=True)
                    ys[r] = jnp.sum(eye * y_col, axis=0, keepdims=True)
                it["st_scr"][it["base"] + h] = st
                it["y_ref"][it["bi"], pl.ds(r0, rows), vl] = jnp.concatenate(ys, axis=0)
        return carry

    lax.fori_loop(0, length // rows, body, 0)


def _gla_solve(items):
    hk = D_HEADS * D_KEY
    probs = []
    for it in items:
        cum = it["cum"]
        last = cum.shape[0] - 1 if it["causal"] else 0
        cl = cum[last:last + 1, :]
        q = it["qk"][:, :hk] * (D_KEY ** -0.5)
        k = it["qk"][:, hk:]
        qe = (q * jnp.exp(cum)).astype(BF16)
        ke = (k * jnp.exp(-cum)).astype(BF16)
        kd = (k * jnp.exp(cl - cum)).astype(BF16)
        el = jnp.exp(cl)
        for h in range(D_HEADS):
            sl = slice(h * D_KEY, (h + 1) * D_KEY)
            probs.append(dict(it=it, h=h, qe=qe[:, sl], ke=ke[:, sl], kd=kd[:, sl], el=el[:, sl],
                              vh=it["v"][:, h * D_VAL:(h + 1) * D_VAL]))
    for p in probs:
        p["att"] = jnp.where(p["it"]["tri"], _nt_dot(p["qe"], p["ke"]), 0.0).astype(BF16)
        p["st"] = p["it"]["st_scr"][p["it"]["base"] + p["h"]]
    for p in probs:
        y = jnp.dot(p["att"], p["vh"].astype(BF16), preferred_element_type=F32)
        p["it"]["y_ref"][p["it"]["bi"], :, p["h"] * D_VAL:(p["h"] + 1) * D_VAL] = (
            y + _nt_dot(p["qe"], p["st"].astype(BF16)))
    for p in probs:
        p["it"]["st_scr"][p["it"]["base"] + p["h"]] = (
            p["st"] * p["el"] + jnp.dot(p["vh"].T.astype(BF16), p["kd"], preferred_element_type=F32))


def _gla_scan_kernel(qkf_ref, vf_ref, smf_ref, qkb_ref, vb_ref, smb_ref, upf_ref, upb_ref, bias_ref,
                     yf_ref, yb_ref, sf_scr, sb_scr, lg_scr):
    @pl.when(pl.program_id(1) == 0)
    def _():
        sf_scr[...] = jnp.zeros_like(sf_scr)
        sb_scr[...] = jnp.zeros_like(sb_scr)

    items = []
    for bi in range(qkf_ref.shape[0]):
        items.append(dict(qk=qkf_ref[bi], v=vf_ref[bi], sm=smf_ref[bi], up=upf_ref[...], bias=bias_ref[0:1, :],
                          causal=True, st_scr=sf_scr, base=bi * D_HEADS, y_ref=yf_ref, bi=bi,
                          qk_ref=qkf_ref, v_ref=vf_ref))
        items.append(dict(qk=qkb_ref[bi], v=vb_ref[bi], sm=smb_ref[bi], up=upb_ref[...], bias=bias_ref[1:2, :],
                          causal=False, st_scr=sb_scr, base=bi * D_HEADS, y_ref=yb_ref, bi=bi,
                          qk_ref=qkb_ref, v_ref=vb_ref))
    _gla_gates(items)
    lowest = items[0]["cum"]
    for it in items[1:]:
        lowest = jnp.minimum(lowest, it["cum"])
    in_range = jnp.min(lowest) > -GLA_SAFE
    pl.when(in_range)(lambda: _gla_solve(items))
    pl.when(jnp.logical_not(in_range))(lambda: _gla_tokenwise(items, lg_scr))


def _gla_scan(qk, v, small, gk_up_f, gk_b_f, gk_up_b, gk_b_b, n_ctx):
    bsz, n, _ = qk.shape
    L = D_CHUNK
    nc, ncc = n // L, n_ctx // L
    bw = _bwd_chunk(ncc, nc)
    hk = D_HEADS * D_KEY
    bias = jnp.zeros((8, hk), F32).at[0].set(gk_b_f).at[1].set(gk_b_b)
    nb = GLA_NB if bsz % GLA_NB == 0 else 1
    fx = lambda w: pl.BlockSpec((nb, L, w), lambda b, s: (b, s, 0))
    bx = lambda w: pl.BlockSpec((nb, L, w), lambda b, s: (b, bw(s), 0))
    return pl.pallas_call(
        _gla_scan_kernel,
        grid=(bsz // nb, nc),
        in_specs=[fx(2 * hk), fx(D_WIDTH), fx(128), bx(2 * hk), bx(D_WIDTH), bx(128),
                  _const_spec((D_LORA, hk)), _const_spec((D_LORA, hk)), _const_spec((8, hk))],
        out_specs=[fx(D_WIDTH), bx(D_WIDTH)],
        out_shape=[jax.ShapeDtypeStruct((bsz, n, D_WIDTH), F32)] * 2,
        scratch_shapes=[pltpu.VMEM((nb * D_HEADS, D_VAL, D_KEY), F32)] * 2 + [pltpu.VMEM((2 * nb, L, hk), F32)],
        compiler_params=_cp("parallel", "arbitrary"),
    )(qk, v, small, qk, v, small, gk_up_f, gk_up_b, bias)


def kernel(x, c, ctx, c_ctx, l0_mod_w, l0_mod_b, l0_norm1, l0_norm2, l0_w_in, l0_w_out, l0_mlp_w1, l0_mlp_w2, l0_lam_q1, l0_lam_k1, l0_lam_q2, l0_lam_k2, l0_subln, l0_mu, l0_w0_f, l0_w2_f, l0_w0_b, l0_w2_b, l0_a0, l0_a2, l0_g2, l0_k_k, l0_k_a, l0_r_k, l0_lnx_w, l0_lnx_b, l1_mod_w, l1_mod_b, l1_norm1, l1_norm2, l1_w_in, l1_w_out, l1_mlp_w1, l1_mlp_w2, l1_conv_w, l1_conv_b, l1_dt_bias_f, l1_a_log_f, l1_dt_bias_b, l1_a_log_b, l1_d_skip, l1_ssm_norm, l1_gk_up_f, l1_gk_b_f, l1_gk_up_b, l1_gk_b_b, l1_gla_norm, final_norm):
    bsz, n_lat, d = x.shape
    n_ctx = ctx.shape[1]
    assert d == D and n_ctx % TM == 0 and n_lat % TM == 0 and n_lat % GRID_W == 0
    nct = n_ctx // TM

    mod0 = _modulation(c, c_ctx, l0_mod_w, l0_mod_b)
    qkv, pb = _inproj0((ctx, x), mod0, l0_norm1, l0_w_in, _rope_tables(n_ctx, n_lat), nct)
    lamv = jnp.stack([l0_lam_q1, l0_lam_k1, l0_lam_q2, l0_lam_k2], axis=0)
    oa = _attention(qkv, lamv, l0_subln, 0, n_ctx)
    yf_b, yb_b, bonus, gate = _rwkv_mix(pb, n_ctx, l0_mu, l0_w0_f, l0_w2_f, l0_w0_b, l0_w2_b, l0_a0, l0_a2, l0_g2,
                                        l0_k_k, l0_k_a, l0_r_k)
    xc = _post((ctx, x), _mix_even, [oa, yf_b, yb_b, bonus, gate],
               [l0_lnx_w.reshape(1, -1), l0_lnx_b.reshape(1, -1), _block_ones(B_WIDTH, B_HEAD)],
               mod0, l0_w_out, l0_norm2, l0_mlp_w1, l0_mlp_w2, None, nct, False)

    mod1 = _modulation(c, c_ctx, l1_mod_w, l1_mod_b)
    z, xact, qk, v, gg, small = _inproj1(xc, mod1, l1_norm1, l1_w_in, l1_conv_w, l1_conv_b, nct)
    yf_c, yb_c = _ssd_scan(xact, small, jnp.swapaxes(small, 1, 2), l1_dt_bias_f, l1_a_log_f,
                           l1_dt_bias_b, l1_a_log_b, n_ctx)
    yf_d, yb_d = _gla_scan(qk, v, small, l1_gk_up_f, l1_gk_b_f, l1_gk_up_b, l1_gk_b_b, n_ctx)
    return _post((xc,), _mix_odd, [yf_c, yb_c, xact, z, yf_d, yb_d, gg],
                 [jnp.repeat(l1_d_skip, C_HEAD).reshape(1, C_WIDTH), l1_ssm_norm.reshape(1, C_WIDTH),
                  l1_gla_norm.reshape(1, D_VAL)],
                 mod1, l1_w_out, l1_norm2, l1_mlp_w1, l1_mlp_w2, final_norm, nct, True)
```

```python
import functools
import math

import jax
import jax.numpy as jnp
from jax import lax
from jax.experimental import pallas as pl
from jax.experimental.pallas import tpu as pltpu

F32 = jnp.float32
BF16 = jnp.bfloat16
HI = lax.Precision.HIGHEST

D = 1024
EPS = 1e-6
GRID_W = 64
ROPE_THETA = 10000.0
MLP_HIDDEN = 4 * D

A_HEADS, A_QK, A_V = 4, 64, 128
A_PROJ = 1536
A_SUBLN_EPS = 1e-5
B_HEADS, B_HEAD = 8, 64
B_WIDTH = 512
B_PROJ = 1792
B_GN_EPS = 64e-5
C_HEADS, C_HEAD, C_GROUPS, C_REP, C_STATE = 8, 64, 2, 4, 128
C_WIDTH = 512
C_CONV = 5
C_CHUNK = 128
C_CONV_DIM = 1024
D_HEADS, D_KEY, D_VAL = 4, 64, 128
D_WIDTH = 512
D_LORA = 16
D_TAU = 16.0
D_CHUNK = 64
ODD_PAD = 3200

TM = 256
TQ = 128
B_CHUNK = 64
B_GROUP = 4
ATTN_KC = 512
SSD_NB = 4
GLA_SAFE = 60.0
GLA_NB = 4
HALO = 8
VMEM_LIMIT = 56 * 1024 * 1024


def _cp(*sem):
    return pltpu.CompilerParams(dimension_semantics=sem, vmem_limit_bytes=VMEM_LIMIT)


def _const_spec(shape):
    nd = len(shape)
    return pl.BlockSpec(shape, lambda *_: (0,) * nd, pipeline_mode=pl.Buffered(1))


def _softplus(x):
    return jnp.maximum(x, 0.0) + jnp.log(1.0 + jnp.exp(-jnp.abs(x)))


def _sigmoid(x):
    return 1.0 / (1.0 + jnp.exp(-x))


def _silu(x):
    return x * _sigmoid(x)


def _ada(x, g, sh, sc):
    ms = jnp.mean(x * x, axis=-1, keepdims=True)
    return (x * lax.rsqrt(ms + EPS) * g) * (1.0 + sc) + sh


def _segsum(x, e, pieces=3):
    hi = x.astype(BF16)
    r1 = x - hi.astype(F32)
    mid = r1.astype(BF16)
    d = lambda a: jnp.dot(a, e, preferred_element_type=F32)
    if pieces == 2:
        return d(hi) + d(mid)
    lo = (r1 - mid.astype(F32)).astype(BF16)
    return d(hi) + d(mid) + d(lo)


def _block_ones(n, seg):
    i = jnp.arange(n) // seg
    return (i[:, None] == i[None, :]).astype(BF16)


def _tile_is_lat(nct):
    return lambda b, i: (b, (i >= nct).astype(jnp.int32), 0, 0)


def _mod_kernel(c_ref, w_ref, b_ref, o_ref):
    s = _silu(c_ref[...])
    o_ref[...] = jnp.dot(s, w_ref[...], precision=HI, preferred_element_type=F32) + b_ref[...]


def _modulation(c, c_ctx, mod_w, mod_b):
    bsz = c.shape[0]
    rows = 8 * ((bsz + 1 + 7) // 8)
    cs = jnp.zeros((rows, D), F32).at[0].set(c_ctx).at[1:bsz + 1].set(c)
    n6 = mod_w.shape[1]
    tn = 1024
    m = pl.pallas_call(
        _mod_kernel,
        grid=(n6 // tn,),
        in_specs=[pl.BlockSpec((rows, D), lambda j: (0, 0)),
                  pl.BlockSpec((D, tn), lambda j: (0, j)),
                  pl.BlockSpec((1, tn), lambda j: (0, j))],
        out_specs=pl.BlockSpec((rows, tn), lambda j: (0, j)),
        out_shape=jax.ShapeDtypeStruct((rows, n6), F32),
        compiler_params=_cp("arbitrary"),
    )(cs, mod_w, mod_b.reshape(1, n6))
    m_c = jnp.broadcast_to(m[0:1], (bsz, n6))
    return jnp.stack([m_c, m[1:bsz + 1]], axis=1)[:, :, None, :]


def _rope_tables(n_ctx, n_lat):
    t = jnp.arange(n_lat)
    row = (t // GRID_W).astype(F32)
    col = (t % GRID_W).astype(F32)
    nf = A_QK // 4
    inv = ROPE_THETA ** (-jnp.arange(nf, dtype=F32) / nf)
    ar = row[:, None] * inv[None, :]
    ac = col[:, None] * inv[None, :]
    z = jnp.zeros_like(ar)
    c64 = jnp.concatenate([jnp.cos(ar), jnp.cos(ar), jnp.cos(ac), jnp.cos(ac)], axis=-1)
    s1 = jnp.concatenate([-jnp.sin(ar), z, -jnp.sin(ac), z], axis=-1)
    s2 = jnp.concatenate([z, jnp.sin(ar), z, jnp.sin(ac)], axis=-1)
    pad = lambda a, v: jnp.concatenate([jnp.full((n_ctx, 128), v, F32), jnp.tile(a, (1, 2))], axis=0)
    return pad(c64, 1.0), pad(s1, 0.0), pad(s2, 0.0)


def _stream_specs(xs, nct, t0=0):
    if len(xs) == 1:
        return [pl.BlockSpec((None, TM, D), lambda b, i: (b, i + t0, 0))]
    return [pl.BlockSpec((None, TM, D), lambda b, i: (b, jnp.minimum(i + t0, nct - 1), 0)),
            pl.BlockSpec((None, TM, D), lambda b, i: (b, jnp.maximum(i + t0 - nct, 0), 0))]


def _stream_tile(x_refs, nct, t0=0):
    if len(x_refs) == 1:
        return x_refs[0][...]
    return jnp.where(pl.program_id(1) + t0 < nct, x_refs[0][...], x_refs[1][...])


def _inproj0_kernel(*refs, n_x, nct):
    x_refs = refs[:n_x]
    mod_ref, g_ref, wa_ref, wb_ref, c_ref, s1_ref, s2_ref, qkv_ref, pb_ref = refs[n_x:]
    mod = mod_ref[...]
    h = _ada(_stream_tile(x_refs, nct), g_ref[...], mod[:, 0:D], mod[:, D:2 * D]).astype(BF16)
    pa = jnp.dot(h, wa_ref[...], preferred_element_type=F32)
    c, s1, s2 = c_ref[...], s1_ref[...], s2_ref[...]
    scale = A_QK ** -0.5 * math.log2(math.e)
    for j in range(8):
        blk = pa[:, j * 128:(j + 1) * 128]
        rot = blk * c + pltpu.roll(blk, 112, 1) * s1 + pltpu.roll(blk, 16, 1) * s2
        if j < 4:
            rot = rot * scale
        qkv_ref[:, j * 128:(j + 1) * 128] = rot.astype(BF16)
    qkv_ref[:, 1024:1536] = pa[:, 1024:1536].astype(BF16)
    pb_ref[...] = jnp.dot(h, wb_ref[...], preferred_element_type=F32)


def _inproj0(xs, mod, norm1, w_in, tabs, nct):
    bsz = xs[0].shape[0]
    n = sum(a.shape[1] for a in xs)
    wa = w_in[:, :A_PROJ].astype(BF16)
    wb = w_in[:, A_PROJ:].astype(BF16)
    tok = lambda w: pl.BlockSpec((None, TM, w), lambda b, i: (b, i, 0))
    tab = pl.BlockSpec((TM, 128), lambda b, i: (i, 0))
    return pl.pallas_call(
        functools.partial(_inproj0_kernel, n_x=len(xs), nct=nct),
        grid=(bsz, n // TM),
        in_specs=_stream_specs(xs, nct) + [pl.BlockSpec((None, None, 1, 6 * D), _tile_is_lat(nct)),
                  _const_spec((1, D)), _const_spec((D, A_PROJ)), _const_spec((D, B_PROJ)),
                  tab, tab, tab],
        out_specs=[tok(A_PROJ), tok(B_PROJ)],
        out_shape=[jax.ShapeDtypeStruct((bsz, n, A_PROJ), BF16),
                   jax.ShapeDtypeStruct((bsz, n, B_PROJ), F32)],
        compiler_params=_cp("parallel", "parallel"),
    )(*xs, mod, norm1.reshape(1, D), wa, wb, *tabs)


def _attn_kernel(lam_ref, q_ref, k_ref, v_ref, subln_ref, o_ref, vt_scr, s0_scr, s1_scr, m0_scr, m1_scr, *,
                 lam_init, n_ctx, nq_ctx, nq):
    qi = pl.program_id(2)
    n_all = k_ref.shape[0]

    @pl.when(qi == 0)
    def _():
        step = ATTN_KC
        for c0 in range(0, n_all, step):
            c1 = min(c0 + step, n_all)
            vt_scr[:, c0:c1] = v_ref[c0:c1, :].astype(F32).T.astype(BF16)

    lv = lam_ref[...]
    lam = (jnp.exp(jnp.sum(lv[0:1] * lv[1:2], axis=-1, keepdims=True))
           - jnp.exp(jnp.sum(lv[2:3] * lv[3:4], axis=-1, keepdims=True)) + lam_init)
    ts = q_ref.shape[0]

    def tree(vals, op):
        while len(vals) > 1:
            vals = [op(vals[i], vals[i + 1]) if i + 1 < len(vals) else vals[i] for i in range(0, len(vals), 2)]
        return vals[0]

    def stage(chunks_cur, chunks_prev, parity):
        s_cur, s_prev = (s0_scr, s1_scr) if parity == 0 else (s1_scr, s0_scr)
        m_cur, m_prev_scr = (m0_scr, m1_scr) if parity == 0 else (m1_scr, m0_scr)
        if chunks_cur:
            q = q_ref[...]
            lane = lax.broadcasted_iota(jnp.int32, q.shape, 1)
            zero = jnp.zeros_like(q)
            q2 = jnp.concatenate([jnp.where(lane < A_QK, q, zero), jnp.where(lane >= A_QK, q, zero)], axis=0)
        if chunks_prev:
            m_prev = m_prev_scr[0:1, :]
        ms, ls, accs = [], [], []
        for ci in range(max(len(chunks_cur), len(chunks_prev))):
            if ci < len(chunks_cur):
                c0, cn = chunks_cur[ci]
                s = _nt_dot(k_ref[c0:c0 + cn, :], q2)
                s_cur[c0:c0 + cn, :] = s
                ms.append(jnp.max(s, axis=0, keepdims=True))
            if ci < len(chunks_prev):
                c0, cn = chunks_prev[ci]
                p = jnp.exp2(s_prev[c0:c0 + cn, :] - m_prev)
                ls.append(jnp.sum(p, axis=0, keepdims=True))
                accs.append(jnp.dot(vt_scr[:, c0:c0 + cn], p.astype(BF16), preferred_element_type=F32))
        if chunks_cur:
            m_cur[...] = jnp.broadcast_to(tree(ms, jnp.maximum), m_cur.shape)
        if chunks_prev:
            on = tree(accs, jnp.add) / tree(ls, jnp.add)
            o = (on[:, :ts] - lam * on[:, ts:]).T
            msq = jnp.mean(o * o, axis=-1, keepdims=True)
            o_ref[...] = (o * lax.rsqrt(msq + A_SUBLN_EPS) * subln_ref[...]) * (1.0 - lam_init)
        else:
            o_ref[...] = jnp.zeros_like(o_ref)

    kc = min(ATTN_KC, n_all - n_ctx)
    ctx_chunks = [(0, n_ctx)]
    all_chunks = ctx_chunks + [(n_ctx + j * kc, kc) for j in range((n_all - n_ctx) // kc)]
    for parity in (0, 1):
        on = lambda cond: pl.when(jnp.logical_and(cond, qi % 2 == parity))
        if parity == 0:
            on(qi == 0)(functools.partial(stage, ctx_chunks, [], parity))
        if nq_ctx > 1:
            on(jnp.logical_and(qi >= 1, qi < nq_ctx))(functools.partial(stage, ctx_chunks, ctx_chunks, parity))
        if nq_ctx % 2 == parity:
            on(qi == nq_ctx)(functools.partial(stage, all_chunks, ctx_chunks, parity))
        on(jnp.logical_and(qi > nq_ctx, qi < nq))(functools.partial(stage, all_chunks, all_chunks, parity))
        if nq % 2 == parity:
            on(qi == nq)(functools.partial(stage, [], all_chunks, parity))


def _attention(qkv, lamv, subln, layer_idx, n_ctx):
    bsz, n, _ = qkv.shape
    lam_init = 0.8 - 0.6 * math.exp(-0.3 * layer_idx)
    nq = n // TQ
    assert n_ctx % TQ == 0 and n_ctx < n
    kern = functools.partial(_attn_kernel, lam_init=lam_init, n_ctx=n_ctx, nq_ctx=n_ctx // TQ, nq=nq)
    return pl.pallas_call(
        kern,
        grid=(bsz, A_HEADS, nq + 1),
        in_specs=[_const_spec((4, A_QK)),
                  pl.BlockSpec((None, TQ, 128), lambda b, h, i: (b, jnp.minimum(i, nq - 1), h)),
                  pl.BlockSpec((None, n, 128), lambda b, h, i: (b, 0, A_HEADS + h)),
                  pl.BlockSpec((None, n, 128), lambda b, h, i: (b, 0, 2 * A_HEADS + h)),
                  _const_spec((1, A_V))],
        out_specs=pl.BlockSpec((None, TQ, 128), lambda b, h, i: (b, jnp.maximum(i - 1, 0), h)),
        out_shape=jax.ShapeDtypeStruct((bsz, n, A_HEADS * A_V), F32),
        scratch_shapes=[pltpu.VMEM((A_V, n), BF16), pltpu.VMEM((n, 2 * TQ), F32), pltpu.VMEM((n, 2 * TQ), F32),
                        pltpu.VMEM((8, 2 * TQ), F32), pltpu.VMEM((8, 2 * TQ), F32)],
        compiler_params=_cp("parallel", "parallel", "arbitrary"),
    )(lamv, qkv, qkv, qkv, subln.reshape(1, A_V))


def _halo_flags(i, nct, nt):
    prev_ok = jnp.logical_and(i != 0, i != nct)
    next_ok = jnp.logical_and(i != nct - 1, i != nt - 1)
    return prev_ok, next_ok


def _rwkv_prep_kernel(p_ref, pp_ref, pn_ref, mu_ref, vec_ref, w2f_ref, w2b_ref, a2_ref, g2_ref, e_ref,
                      r_ref, wf_ref, wb_ref, k_ref, v_ref, kk_ref, kb_ref, bonus_ref, g_ref, *, nct, nt):
    i = pl.program_id(1)
    f = p_ref[...]
    tm = f.shape[0]
    prev_ok, next_ok = _halo_flags(i, nct, nt)
    hp = jnp.where(prev_ok, pp_ref[HALO - 1:HALO, :], 0.0)
    hn = jnp.where(next_ok, pn_ref[0:1, :], 0.0)
    row = lax.broadcasted_iota(jnp.int32, f.shape, 0)
    prev = jnp.where(row == 0, hp, pltpu.roll(f, 1, 0))
    nxt = jnp.where(row == tm - 1, hn, pltpu.roll(f, tm - 1, 0))
    fm = f + (0.5 * (prev + nxt) - f) * mu_ref[...]
    r = fm[:, 0:512]
    k = fm[:, 512:1024]
    v = fm[:, 1024:1536]
    wd = fm[:, 1536:1600]
    ad = fm[:, 1600:1664]
    gd = fm[:, 1664:1792]
    w0f, w0b, a0 = vec_ref[0:1, :], vec_ref[1:2, :], vec_ref[2:3, :]
    k_k, k_a, r_k = vec_ref[3:4, :], vec_ref[4:5, :], vec_ref[5:6, :]
    e = e_ref[...]
    mm = lambda a, w_ref: jnp.dot(a.astype(BF16), w_ref[...], preferred_element_type=F32)
    a = _sigmoid(a0 + mm(ad, a2_ref))
    tw = jnp.tanh(wd)
    decay = lambda w0, w2_ref: -jnp.exp(-_softplus(-(w0 + mm(tw, w2_ref))) - 0.5)
    kkr = k * k_k
    kk = kkr / jnp.maximum(jnp.sqrt(_segsum(kkr * kkr, e, 2)), 1e-12)
    k2 = k * (1.0 + (a - 1.0) * k_a)
    r_ref[...] = r.astype(BF16)
    wf_ref[...] = decay(w0f, w2f_ref)
    wb_ref[...] = decay(w0b, w2b_ref)
    k_ref[...] = k2.astype(BF16)
    v_ref[...] = v.astype(BF16)
    kk_ref[...] = kk.astype(BF16)
    kb_ref[...] = (kk * a).astype(BF16)
    bonus_ref[...] = (_segsum(r * k2 * r_k, e, 2) * v).astype(BF16)
    g_ref[...] = mm(_sigmoid(gd), g2_ref).astype(BF16)


def _rwkv_prep(pb, mu, vecs, w2_f, w2_b, a2, g2, nct):
    bsz, n, _ = pb.shape
    nt = n // TM
    hb = TM // HALO
    tok = lambda w: pl.BlockSpec((None, TM, w), lambda b, i: (b, i, 0))
    kern = functools.partial(_rwkv_prep_kernel, nct=nct, nt=nt)
    outs = pl.pallas_call(
        kern,
        grid=(bsz, nt),
        in_specs=[tok(B_PROJ),
                  pl.BlockSpec((None, HALO, B_PROJ), lambda b, i: (b, jnp.maximum(i * hb - 1, 0), 0)),
                  pl.BlockSpec((None, HALO, B_PROJ), lambda b, i: (b, jnp.minimum((i + 1) * hb, nt * hb - 1), 0)),
                  _const_spec((1, B_PROJ)), _const_spec((8, B_WIDTH)),
                  _const_spec((64, B_WIDTH)), _const_spec((64, B_WIDTH)), _const_spec((64, B_WIDTH)),
                  _const_spec((128, B_WIDTH)), _const_spec((B_WIDTH, B_WIDTH))],
        out_specs=[tok(B_WIDTH)] * 9,
        out_shape=[jax.ShapeDtypeStruct((bsz, n, B_WIDTH), dt)
                   for dt in (BF16, F32, F32, BF16, BF16, BF16, BF16, BF16, BF16)],
        compiler_params=_cp("parallel", "parallel"),
    )(pb, pb, pb, mu.reshape(1, B_PROJ), vecs, w2_f.astype(BF16), w2_b.astype(BF16), a2.astype(BF16),
      g2.astype(BF16), _block_ones(B_WIDTH, B_HEAD))
    return outs


def _cumsum3(tri_b, x):
    hi = x.astype(BF16)
    r1 = x - hi.astype(F32)
    mid = r1.astype(BF16)
    lo = (r1 - mid.astype(F32)).astype(BF16)
    d = lambda a: jnp.dot(tri_b, a, preferred_element_type=F32)
    return d(hi) + d(mid) + d(lo)


def _head_spread(heads, width):
    row = lax.broadcasted_iota(jnp.int32, (heads, heads * width), 0)
    col = lax.broadcasted_iota(jnp.int32, (heads, heads * width), 1)
    return (row == col // width).astype(BF16)


def _bdot(a, b):
    return jnp.dot(a.astype(BF16), b.astype(BF16), preferred_element_type=F32)


def _rwkv_dir(r, lw, k, v, kk, kb, s_scr, y_ref, causal):
    length = r.shape[0]
    assert length == B_HEAD
    gw = B_GROUP * B_HEAD
    ii = lax.broadcasted_iota(jnp.int32, (length, gw), 0)
    jj = lax.broadcasted_iota(jnp.int32, (length, gw), 1) % length
    tri = (ii >= jj) if causal else (ii <= jj)
    strict = jnp.logical_and(tri, ii != jj)
    lvl = []
    m = 1
    while m < length:
        same2 = (ii // (2 * m)) == (jj // (2 * m))
        diff1 = (ii // m) != (jj // m)
        lvl.append(jnp.logical_and(jnp.logical_and(same2, diff1), strict))
        m *= 2
    tri_sq, _ = _scan_masks(length, causal)
    cum = _cumsum3(tri_sq.astype(BF16), lw)
    last = length - 1 if causal else 0
    cl = cum[last:last + 1, :]
    gi = jnp.exp(-cum)
    gl = jnp.exp(cl - cum)
    kg = kk * jnp.exp(cum - lw)
    rg = r * jnp.exp(cum)
    ki, bi = k * gi, kb * gi
    kd, bd = k * gl, kb * gl
    el = jnp.exp(cl)
    probs = []
    for g in range(r.shape[1] // gw):
        sl = slice(g * gw, (g + 1) * gw)
        probs.append(dict(
            sl=sl, tri=tri, strict=strict, lvl=lvl, s_scr=s_scr, y_ref=y_ref, g=g,
            lhs=jnp.concatenate([kg[:, sl], rg[:, sl]], axis=0).astype(BF16),
            bi=bi[:, sl], ki=ki[:, sl], v=v[:, sl],
            kbd=jnp.concatenate([kd[:, sl], -bd[:, sl]], axis=0).astype(BF16),
            el=el[:, sl]))
    return probs


def _rwkv_solve(probs):
    length = probs[0]["v"].shape[0]
    gw = probs[0]["v"].shape[1]
    hrow = lax.broadcasted_iota(jnp.int32, (gw, gw), 0) // B_HEAD
    hcol = lax.broadcasted_iota(jnp.int32, (gw, gw), 1) // B_HEAD
    same_head = hrow == hcol
    same_head_b = same_head.astype(BF16)

    def bd(x):
        xb = x.astype(BF16)
        return jnp.concatenate([xb] * B_GROUP, axis=0) * same_head_b

    mm = lambda a, w: jnp.dot(a.astype(BF16), w, preferred_element_type=F32)
    for p in probs:
        p["sc_b"] = _nt_dot(p["lhs"], bd(p["bi"]))
        p["sc_k"] = _nt_dot(p["lhs"], bd(p["ki"]))
    for p in probs:
        p["a_kb"] = jnp.where(p["strict"], p["sc_b"][:length], 0.0)
        p["a_rb"] = jnp.where(p["tri"], p["sc_b"][length:], 0.0).astype(BF16)
        p["a_k2"] = jnp.concatenate([jnp.where(p["strict"], p["sc_k"][:length], 0.0),
                                     jnp.where(p["tri"], p["sc_k"][length:], 0.0)], axis=0).astype(BF16)
        p["nn"] = -jnp.where(p["lvl"][0], p["a_kb"], 0.0)
        p["s0"] = p["s_scr"][p["g"]]
    for p in probs:
        p["hs"] = _nt_dot(p["lhs"], bd(p["s0"])) + mm(p["a_k2"], bd(p["v"]))
    for lv in range(1, len(probs[0]["lvl"])):
        for p in probs:
            aoff = jnp.where(p["lvl"][lv], p["a_kb"], 0.0)
            p["x"] = aoff + mm(p["nn"], bd(aoff))
        for p in probs:
            p["nn"] = p["nn"] - (p["x"] + mm(p["x"], bd(p["nn"])))
    for p in probs:
        rhs = p["hs"][:length]
        p["u"] = rhs + mm(p["nn"], bd(rhs))
    for p in probs:
        p["y_ref"][:, p["sl"]] = p["hs"][length:] - mm(p["a_rb"], bd(p["u"]))
    for p in probs:
        vu = jnp.concatenate([p["v"].astype(F32), p["u"]], axis=0)
        z = jnp.dot(vu.T.astype(BF16), p["kbd"], preferred_element_type=F32)
        z = jnp.where(same_head, z, 0.0)
        fold = z[0:B_HEAD]
        for j in range(1, B_GROUP):
            fold = fold + z[j * B_HEAD:(j + 1) * B_HEAD]
        p["s_scr"][p["g"]] = p["s0"] * p["el"] + fold


def _rwkv_chunk_kernel(*refs):
    fwd, bwd = refs[0:6], refs[6:12]
    yf_ref, yb_ref, sf_scr, sb_scr = refs[12:16]

    @pl.when(pl.program_id(1) == 0)
    def _():
        sf_scr[...] = jnp.zeros_like(sf_scr)
        sb_scr[...] = jnp.zeros_like(sb_scr)

    probs = _rwkv_dir(*[x[...] for x in fwd], sf_scr, yf_ref, True)
    probs += _rwkv_dir(*[x[...] for x in bwd], sb_scr, yb_ref, False)
    _rwkv_solve(probs)


def _rwkv_chunk_scan(r, lwf, lwb, k, v, kk, kb, n_ctx):
    bsz, n, width = r.shape
    L = B_CHUNK
    nc, ncc = n // L, n_ctx // L
    bw = _bwd_chunk(ncc, nc)
    fx = pl.BlockSpec((None, L, width), lambda b, s: (b, s, 0))
    bx = pl.BlockSpec((None, L, width), lambda b, s: (b, bw(s), 0))
    return pl.pallas_call(
        _rwkv_chunk_kernel,
        grid=(bsz, nc),
        in_specs=[fx] * 6 + [bx] * 6,
        out_specs=[fx, bx],
        out_shape=[jax.ShapeDtypeStruct((bsz, n, width), F32)] * 2,
        scratch_shapes=[pltpu.VMEM((B_HEADS // B_GROUP, B_HEAD, B_GROUP * B_HEAD), F32)] * 2,
        compiler_params=_cp("parallel", "arbitrary"),
    )(r, lwf, k, v, kk, kb, r, lwb, k, v, kk, kb)


def _rwkv_mix(pb, n_ctx, mu, w0_f, w2_f, w0_b, w2_b, a0, a2, g2, k_k, k_a, r_k):
    zero = jnp.zeros_like(w0_f)
    vecs = jnp.stack([w0_f, w0_b, a0, k_k, k_a, r_k, zero, zero], axis=0)
    r, lwf, lwb, k2, v, kk, kb, bonus, g = _rwkv_prep(pb, mu, vecs, w2_f, w2_b, a2, g2, n_ctx // TM)
    yf, yb = _rwkv_chunk_scan(r, lwf, lwb, k2, v, kk, kb, n_ctx)
    return yf, yb, bonus, g


def _mix_even(tok, cst):
    oa, yf, yb, bonus, g = tok
    lnw, lnb, e = cst
    y = yf + yb
    d = y - _segsum(y, e, 2) * (1.0 / B_HEAD)
    var = _segsum(d * d, e, 2) * (1.0 / B_HEAD)
    return oa,(d * lax.rsqrt(var + B_GN_EPS) * lnw + lnb + bonus) * g


def _mix_odd(tok, cst):
    yfc, ybc, xs, z, yfd, ybd, gg = tok
    dsk, ngc, ngd = cst
    y = (yfc + ybc + xs * dsk) * _silu(z)
    gw = C_WIDTH // C_GROUPS
    oc = []
    for g in range(C_GROUPS):
        blk = y[:, g * gw:(g + 1) * gw]
        ms = jnp.mean(blk * blk, axis=-1, keepdims=True)
        oc.append(blk * lax.rsqrt(ms + EPS) * ngc[:, g * gw:(g + 1) * gw])
    y = yfd + ybd
    gate = _silu(gg)
    od = []
    for h in range(D_HEADS):
        sl = slice(h * D_VAL, (h + 1) * D_VAL)
        blk = y[:, sl]
        ms = jnp.mean(blk * blk, axis=-1, keepdims=True)
        od.append(blk * lax.rsqrt(ms + EPS) * ngd * gate[:, sl])
    return jnp.concatenate(oc, axis=-1), jnp.concatenate(od, axis=-1)


def _post_kernel(*refs, mix_fn, n_x, nct, t0, n_tok, n_cst, final):
    x_refs, refs = refs[:n_x], refs[n_x:]
    tok = [r[...] for r in refs[:n_tok]]
    cst = [r[...] for r in refs[n_tok:n_tok + n_cst]]
    mod_ref, woa_ref, wob_ref, n2_ref, w1_ref, w2_ref, fn_ref, out_ref = refs[n_tok + n_cst:]
    mod = mod_ref[...]
    g1, sh2 = mod[:, 2 * D:3 * D], mod[:, 3 * D:4 * D]
    sc2, g2 = mod[:, 4 * D:5 * D], mod[:, 5 * D:6 * D]
    o1, o2 = mix_fn(tok, cst)
    mix = (jnp.dot(o1.astype(BF16), woa_ref[...], preferred_element_type=F32)
           + jnp.dot(o2.astype(BF16), wob_ref[...], preferred_element_type=F32))
    x1 = _stream_tile(x_refs, nct, t0) + g1 * mix
    h = _ada(x1, n2_ref[...], sh2, sc2).astype(BF16)
    hid = jnp.maximum(jnp.dot(h, w1_ref[...], preferred_element_type=F32), 0.0)
    hid = (hid * hid).astype(BF16)
    x2 = x1 + g2 * jnp.dot(hid, w2_ref[...], preferred_element_type=F32)
    if final:
        ms = jnp.mean(x2 * x2, axis=-1, keepdims=True)
        x2 = x2 * lax.rsqrt(ms + EPS) * fn_ref[...]
    out_ref[...] = x2


def _post(xs, mix_fn, tok_in, cst_in, mod, w_out, norm2, mlp_w1, mlp_w2, final_norm, nct, lat_only):
    bsz = xs[0].shape[0]
    n = sum(a.shape[1] for a in xs)
    half = w_out.shape[0] // 2
    t0 = nct if lat_only else 0
    nt = n // TM - t0
    tok = lambda w: pl.BlockSpec((None, TM, w), lambda b, i: (b, i + t0, 0))
    final = final_norm is not None
    fn = (final_norm if final else jnp.ones((D,), F32)).reshape(1, D)
    kern = functools.partial(_post_kernel, mix_fn=mix_fn, n_x=len(xs), nct=nct, t0=t0, n_tok=len(tok_in),
                             n_cst=len(cst_in), final=final)
    return pl.pallas_call(
        kern,
        grid=(bsz, nt),
        in_specs=_stream_specs(xs, nct, t0) + [tok(half)] * len(tok_in) + [_const_spec(a.shape) for a in cst_in]
                 + [pl.BlockSpec((None, None, 1, 6 * D), lambda b, i: (b, (i + t0 >= nct).astype(jnp.int32), 0, 0)),
                    _const_spec((half, D)), _const_spec((half, D)), _const_spec((1, D)),
                    _const_spec((D, MLP_HIDDEN)), _const_spec((MLP_HIDDEN, D)), _const_spec((1, D))],
        out_specs=pl.BlockSpec((None, TM, D), lambda b, i: (b, i, 0)),
        out_shape=jax.ShapeDtypeStruct((bsz, nt * TM, D), F32),
        compiler_params=_cp("parallel", "parallel"),
    )(*xs, *tok_in, *cst_in, mod, w_out[:half].astype(BF16), w_out[half:].astype(BF16), norm2.reshape(1, D),
      mlp_w1.astype(BF16), mlp_w2.astype(BF16), fn)


def _inproj1_kernel(x_ref, xp_ref, xn_ref, mod_ref, g_ref, w_ref, cw_ref, cb_ref,
                    z_ref, xact_ref, qk_ref, v_ref, gg_ref, sm_ref, *, nct, nt):
    mod = mod_ref[...]
    tm = x_ref.shape[0]
    ne = tm + 2 * HALO
    x_ext = jnp.concatenate([xp_ref[...], x_ref[...], xn_ref[...]], axis=0)
    h = _ada(x_ext, g_ref[...], mod[:, 0:D], mod[:, D:2 * D]).astype(BF16)
    xbc_ext = jnp.dot(h, w_ref[:, 512:1536], preferred_element_type=F32)
    h_main = h[HALO:HALO + tm]
    for ref, c0, c1 in ((z_ref, 0, 512), (qk_ref, 1536, 2048), (v_ref, 2048, 2560), (gg_ref, 2560, 3072),
                        (sm_ref, 3072, 3200)):
        ref[...] = jnp.dot(h_main, w_ref[:, c0:c1], preferred_element_type=F32)
    xbc = xbc_ext[HALO:HALO + tm]
    prev_ok, next_ok = _halo_flags(pl.program_id(1), nct, nt)
    row = lax.broadcasted_iota(jnp.int32, (ne, C_CONV_DIM), 0)
    dead = jnp.logical_or(jnp.logical_and(row < HALO, jnp.logical_not(prev_ok)),
                          jnp.logical_and(row >= HALO + tm, jnp.logical_not(next_ok)))
    ext = jnp.where(dead, 0.0, xbc_ext)
    acc = cb_ref[...] + cw_ref[2:3, :] * xbc
    for j in (0, 1, 3, 4):
        shifted = pltpu.roll(ext, (C_CONV // 2 - j) % ne, 0)[HALO:HALO + tm]
        acc = acc + cw_ref[j:j + 1, :] * shifted
    xact_ref[...] = _silu(acc)


def _halo_specs(width, nt):
    hb = TM // HALO
    return [pl.BlockSpec((None, HALO, width), lambda b, i: (b, jnp.maximum(i * hb - 1, 0), 0)),
            pl.BlockSpec((None, HALO, width), lambda b, i: (b, jnp.minimum((i + 1) * hb, nt * hb - 1), 0))]


def _inproj1(xc, mod, norm1, w_in, conv_w, conv_b, nct):
    bsz, n, _ = xc.shape
    nt = n // TM
    w8 = jnp.zeros((8, C_CONV_DIM), F32).at[:C_CONV].set(conv_w)
    c0 = C_WIDTH + C_CONV_DIM
    d0 = c0 + 2 * C_HEADS
    qk_w = 2 * D_HEADS * D_KEY
    w = jnp.concatenate([w_in[:, :c0], w_in[:, d0:d0 + qk_w + 2 * D_WIDTH], w_in[:, c0:d0],
                         w_in[:, d0 + qk_w + 2 * D_WIDTH:],
                         jnp.zeros((D, 128 - 2 * C_HEADS - D_LORA), F32)], axis=1).astype(BF16)
    tok = lambda wd: pl.BlockSpec((None, TM, wd), lambda b, i: (b, i, 0))
    widths = [512, 1024, 512, 512, 512, 128]
    return pl.pallas_call(
        functools.partial(_inproj1_kernel, nct=nct, nt=nt),
        grid=(bsz, nt),
        in_specs=[tok(D)] + _halo_specs(D, nt)
                 + [pl.BlockSpec((None, None, 1, 6 * D), _tile_is_lat(nct)),
                    _const_spec((1, D)), _const_spec((D, ODD_PAD)),
                    _const_spec((8, C_CONV_DIM)), _const_spec((1, C_CONV_DIM))],
        out_specs=[tok(wd) for wd in widths],
        out_shape=[jax.ShapeDtypeStruct((bsz, n, wd), F32) for wd in widths],
        compiler_params=_cp("parallel", "parallel"),
    )(xc, xc, xc, mod, norm1.reshape(1, D), w, w8, conv_b.reshape(1, C_CONV_DIM))


def _scan_masks(length, causal):
    ii = lax.broadcasted_iota(jnp.int32, (length, length), 0)
    jj = lax.broadcasted_iota(jnp.int32, (length, length), 1)
    tri = (ii >= jj) if causal else (ii <= jj)
    tri_t = (ii <= jj) if causal else (ii >= jj)
    return tri, tri_t


def _nt_dot(a, b):
    return lax.dot_general(a, b, (((1,), (1,)), ((), ())), preferred_element_type=F32)


def _ssd_dir(xa, sm, sm_t, col0, prm, prm_t, h_scr, y_ref, causal, bi):
    length = xa.shape[0]
    dtb, alog = prm[0:1, col0:col0 + C_HEADS], prm[1:2, col0:col0 + C_HEADS]
    dtb_t, alog_t = prm_t[col0:col0 + C_HEADS, 0:1], prm_t[col0:col0 + C_HEADS, 1:2]
    dt = _softplus(sm[:, col0:col0 + C_HEADS] + dtb)
    la = dt * (-jnp.exp(alog))
    la_t = _softplus(sm_t[col0:col0 + C_HEADS, :] + dtb_t) * (-jnp.exp(alog_t))
    tri, tri_t = _scan_masks(length, causal)
    cum = jnp.dot(tri.astype(F32), la, precision=HI, preferred_element_type=F32)
    cum_t = jnp.dot(la_t, tri_t.astype(F32), precision=HI, preferred_element_type=F32)
    last = length - 1 if causal else 0
    spread = lambda a, w: _segsum(a, _head_spread(C_HEADS, w))
    xdt_all = xa[:, :C_WIDTH] * spread(dt, C_HEAD)
    xw_all = xdt_all * spread(jnp.exp(cum[last:last + 1, :] - cum), C_HEAD)
    ech_all = spread(jnp.exp(cum), C_HEAD)
    cum_all = spread(cum, length)
    ecl_t = jnp.exp(cum_t[:, last:last + 1])
    gw = C_REP * C_HEAD
    groups, probs = [], []
    for g in range(C_GROUPS):
        bg = xa[:, C_WIDTH + g * C_STATE:C_WIDTH + (g + 1) * C_STATE].astype(BF16)
        cg = xa[:, C_WIDTH + (C_GROUPS + g) * C_STATE:C_WIDTH + (C_GROUPS + g + 1) * C_STATE].astype(BF16)
        ecl_rows = jnp.concatenate([jnp.broadcast_to(ecl_t[g * C_REP + rr:g * C_REP + rr + 1, :], (C_HEAD, 1))
                                    for rr in range(C_REP)], axis=0)
        grp = dict(bg=bg, cg=cg, gi=bi * C_GROUPS + g, h_scr=h_scr, ecl_rows=ecl_rows,
                   xw=xw_all[:, g * gw:(g + 1) * gw], ech=ech_all[:, g * gw:(g + 1) * gw])
        groups.append(grp)
        for rr in range(C_REP):
            h = g * C_REP + rr
            probs.append(dict(
                grp=grp, tri=tri, diff=cum_all[:, h * length:(h + 1) * length] - cum_t[h:h + 1, :],
                xdt=xdt_all[:, h * C_HEAD:(h + 1) * C_HEAD].astype(BF16)))
    return groups, probs, (y_ref, bi)


def _ssd_solve(dirs):
    groups = [g for gs, _, _ in dirs for g in gs]
    probs = [p for _, ps, _ in dirs for p in ps]
    for grp in groups:
        grp["cb"] = _nt_dot(grp["cg"], grp["bg"])
        grp["hs"] = grp["h_scr"][grp["gi"]]
    for p in probs:
        lmat = jnp.exp(jnp.where(p["tri"], p["diff"], -jnp.inf))
        p["m"] = (p["grp"]["cb"] * lmat).astype(BF16)
    for grp in groups:
        grp["y_inter"] = _nt_dot(grp["cg"], grp["hs"].astype(BF16)) * grp["ech"]
    for p in probs:
        p["y"] = jnp.dot(p["m"], p["xdt"], preferred_element_type=F32)
    for grp in groups:
        xw_t = grp["xw"].T.astype(BF16)
        grp["h_scr"][grp["gi"]] = (grp["hs"] * grp["ecl_rows"]
                                   + jnp.dot(xw_t, grp["bg"], preferred_element_type=F32))
    for gs, ps, (y_ref, bi) in dirs:
        y_ref[bi] = (jnp.concatenate([p["y"] for p in ps], axis=-1)
                     + jnp.concatenate([g["y_inter"] for g in gs], axis=-1))


def _ssd_scan_kernel(xf_ref, smf_ref, smtf_ref, xb_ref, smb_ref, smtb_ref, prm_ref, prmt_ref,
                     yf_ref, yb_ref, hf_scr, hb_scr):
    @pl.when(pl.program_id(1) == 0)
    def _():
        hf_scr[...] = jnp.zeros_like(hf_scr)
        hb_scr[...] = jnp.zeros_like(hb_scr)

    prm, prm_t = prm_ref[...], prmt_ref[...]
    dirs = []
    for bi in range(xf_ref.shape[0]):
        dirs.append(_ssd_dir(xf_ref[bi], smf_ref[bi], smtf_ref[bi], 0, prm, prm_t, hf_scr, yf_ref, True, bi))
        dirs.append(_ssd_dir(xb_ref[bi], smb_ref[bi], smtb_ref[bi], C_HEADS, prm, prm_t, hb_scr, yb_ref, False, bi))
    _ssd_solve(dirs)


def _bwd_chunk(ncc, nc):
    return lambda s: jnp.where(s < ncc, ncc - 1 - s, nc - 1 - s + ncc)


def _ssd_scan(xact, small, small_t, dt_bias_f, a_log_f, dt_bias_b, a_log_b, n_ctx):
    bsz, n, cdim = xact.shape
    L = C_CHUNK
    nc, ncc = n // L, n_ctx // L
    bw = _bwd_chunk(ncc, nc)
    prm = jnp.zeros((8, 128), F32).at[0, :16].set(jnp.concatenate([dt_bias_f, dt_bias_b]))
    prm = prm.at[1, :16].set(jnp.concatenate([a_log_f, a_log_b]))
    nb = SSD_NB if bsz % SSD_NB == 0 else 1
    fx = lambda w: pl.BlockSpec((nb, L, w), lambda b, s: (b, s, 0))
    bx = lambda w: pl.BlockSpec((nb, L, w), lambda b, s: (b, bw(s), 0))
    ft = pl.BlockSpec((nb, 128, L), lambda b, s: (b, 0, s))
    bt = pl.BlockSpec((nb, 128, L), lambda b, s: (b, 0, bw(s)))
    return pl.pallas_call(
        _ssd_scan_kernel,
        grid=(bsz // nb, nc),
        in_specs=[fx(cdim), fx(128), ft, bx(cdim), bx(128), bt, _const_spec((8, 128)), _const_spec((128, 8))],
        out_specs=[fx(C_WIDTH), bx(C_WIDTH)],
        out_shape=[jax.ShapeDtypeStruct((bsz, n, C_WIDTH), F32)] * 2,
        scratch_shapes=[pltpu.VMEM((nb * C_GROUPS, C_REP * C_HEAD, C_STATE), F32)] * 2,
        compiler_params=_cp("parallel", "arbitrary"),
    )(xact, small, small_t, xact, small, small_t, prm, prm.T)


def _gla_gates(items):
    for it in items:
        gd = it["sm"][:, 2 * C_HEADS:2 * C_HEADS + D_LORA]
        it["zz"] = jnp.dot(gd, it["up"], precision=HI, preferred_element_type=F32) + it["bias"]
    for it in items:
        it["lg"] = -_softplus(-it["zz"]) * (1.0 / D_TAU)
        it["tri"], _ = _scan_masks(it["lg"].shape[0], it["causal"])
        it["cum"] = _cumsum3(it["tri"].astype(BF16), it["lg"])


def _gla_tokenwise(items, lg_scr):
    length = lg_scr.shape[1]
    hk = D_HEADS * D_KEY
    eye = (lax.broadcasted_iota(jnp.int32, (D_VAL, D_VAL), 0)
           == lax.broadcasted_iota(jnp.int32, (D_VAL, D_VAL), 1)).astype(F32)
    for ii, it in enumerate(items):
        lg_scr[ii] = it["lg"]

    rows = 8

    def body(step, carry):
        for ii, it in enumerate(items):
            grp = step if it["causal"] else length // rows - 1 - step
            r0 = pl.multiple_of(grp * rows, rows)
            g_blk = jnp.exp(lg_scr[ii, pl.ds(r0, rows), :])
            qk_blk = it["qk_ref"][it["bi"], pl.ds(r0, rows), :]
            v_blk = it["v_ref"][it["bi"], pl.ds(r0, rows), :]
            for h in range(D_HEADS):
                sl = slice(h * D_KEY, (h + 1) * D_KEY)
                kl = slice(hk + h * D_KEY, hk + (h + 1) * D_KEY)
                vl = slice(h * D_VAL, (h + 1) * D_VAL)
                st = it["st_scr"][it["base"] + h]
                ys = [None] * rows
                for r in (range(rows) if it["causal"] else range(rows - 1, -1, -1)):
                    v_col = jnp.sum(eye * v_blk[r:r + 1, vl], axis=1, keepdims=True)
                    st = st * g_blk[r:r + 1, sl] + v_col * qk_blk[r:r + 1, kl]
                    y_col = jnp.sum(st * (qk_blk[r:r + 1, sl] * (D_KEY ** -0.5)), axis=1, keepdims=True)
                    ys[r] = jnp.sum(eye * y_col, axis=0, keepdims=True)
                it["st_scr"][it["base"] + h] = st
                it["y_ref"][it["bi"], pl.ds(r0, rows), vl] = jnp.concatenate(ys, axis=0)
        return carry

    lax.fori_loop(0, length // rows, body, 0)


def _gla_solve(items):
    hk = D_HEADS * D_KEY
    probs = []
    for it in items:
        cum = it["cum"]
        last = cum.shape[0] - 1 if it["causal"] else 0
        cl = cum[last:last + 1, :]
        q = it["qk"][:, :hk] * (D_KEY ** -0.5)
        k = it["qk"][:, hk:]
        qe = (q * jnp.exp(cum)).astype(BF16)
        ke = (k * jnp.exp(-cum)).astype(BF16)
        kd = (k * jnp.exp(cl - cum)).astype(BF16)
        el = jnp.exp(cl)
        for h in range(D_HEADS):
            sl = slice(h * D_KEY, (h + 1) * D_KEY)
            probs.append(dict(it=it, h=h, qe=qe[:, sl], ke=ke[:, sl], kd=kd[:, sl], el=el[:, sl],
                              vh=it["v"][:, h * D_VAL:(h + 1) * D_VAL]))
    for p in probs:
        p["att"] = jnp.where(p["it"]["tri"], _nt_dot(p["qe"], p["ke"]), 0.0).astype(BF16)
        p["st"] = p["it"]["st_scr"][p["it"]["base"] + p["h"]]
    for p in probs:
        y = jnp.dot(p["att"], p["vh"].astype(BF16), preferred_element_type=F32)
        p["it"]["y_ref"][p["it"]["bi"], :, p["h"] * D_VAL:(p["h"] + 1) * D_VAL] = (
            y + _nt_dot(p["qe"], p["st"].astype(BF16)))
    for p in probs:
        p["it"]["st_scr"][p["it"]["base"] + p["h"]] = (
            p["st"] * p["el"] + jnp.dot(p["vh"].T.astype(BF16), p["kd"], preferred_element_type=F32))


def _gla_scan_kernel(qkf_ref, vf_ref, smf_ref, qkb_ref, vb_ref, smb_ref, upf_ref, upb_ref, bias_ref,
                     yf_ref, yb_ref, sf_scr, sb_scr, lg_scr):
    @pl.when(pl.program_id(1) == 0)
    def _():
        sf_scr[...] = jnp.zeros_like(sf_scr)
        sb_scr[...] = jnp.zeros_like(sb_scr)

    items = []
    for bi in range(qkf_ref.shape[0]):
        items.append(dict(qk=qkf_ref[bi], v=vf_ref[bi], sm=smf_ref[bi], up=upf_ref[...], bias=bias_ref[0:1, :],
                          causal=True, st_scr=sf_scr, base=bi * D_HEADS, y_ref=yf_ref, bi=bi,
                          qk_ref=qkf_ref, v_ref=vf_ref))
        items.append(dict(qk=qkb_ref[bi], v=vb_ref[bi], sm=smb_ref[bi], up=upb_ref[...], bias=bias_ref[1:2, :],
                          causal=False, st_scr=sb_scr, base=bi * D_HEADS, y_ref=yb_ref, bi=bi,
                          qk_ref=qkb_ref, v_ref=vb_ref))
    _gla_gates(items)
    ends = [it["cum"][-1:, :] if it["causal"] else it["cum"][0:1, :] for it in items]
    lowest = ends[0]
    for e in ends[1:]:
        lowest = jnp.minimum(lowest, e)
    in_range = jnp.min(lowest) > -GLA_SAFE
    pl.when(in_range)(lambda: _gla_solve(items))
    pl.when(jnp.logical_not(in_range))(lambda: _gla_tokenwise(items, lg_scr))


def _gla_scan(qk, v, small, gk_up_f, gk_b_f, gk_up_b, gk_b_b, n_ctx):
    bsz, n, _ = qk.shape
    L = D_CHUNK
    nc, ncc = n // L, n_ctx // L
    bw = _bwd_chunk(ncc, nc)
    hk = D_HEADS * D_KEY
    bias = jnp.zeros((8, hk), F32).at[0].set(gk_b_f).at[1].set(gk_b_b)
    nb = GLA_NB if bsz % GLA_NB == 0 else 1
    fx = lambda w: pl.BlockSpec((nb, L, w), lambda b, s: (b, s, 0))
    bx = lambda w: pl.BlockSpec((nb, L, w), lambda b, s: (b, bw(s), 0))
    return pl.pallas_call(
        _gla_scan_kernel,
        grid=(bsz // nb, nc),
        in_specs=[fx(2 * hk), fx(D_WIDTH), fx(128), bx(2 * hk), bx(D_WIDTH), bx(128),
                  _const_spec((D_LORA, hk)), _const_spec((D_LORA, hk)), _const_spec((8, hk))],
        out_specs=[fx(D_WIDTH), bx(D_WIDTH)],
        out_shape=[jax.ShapeDtypeStruct((bsz, n, D_WIDTH), F32)] * 2,
        scratch_shapes=[pltpu.VMEM((nb * D_HEADS, D_VAL, D_KEY), F32)] * 2 + [pltpu.VMEM((2 * nb, L, hk), F32)],
        compiler_params=_cp("parallel", "arbitrary"),
    )(qk, v, small, qk, v, small, gk_up_f, gk_up_b, bias)


def kernel(x, c, ctx, c_ctx, l0_mod_w, l0_mod_b, l0_norm1, l0_norm2, l0_w_in, l0_w_out, l0_mlp_w1, l0_mlp_w2, l0_lam_q1, l0_lam_k1, l0_lam_q2, l0_lam_k2, l0_subln, l0_mu, l0_w0_f, l0_w2_f, l0_w0_b, l0_w2_b, l0_a0, l0_a2, l0_g2, l0_k_k, l0_k_a, l0_r_k, l0_lnx_w, l0_lnx_b, l1_mod_w, l1_mod_b, l1_norm1, l1_norm2, l1_w_in, l1_w_out, l1_mlp_w1, l1_mlp_w2, l1_conv_w, l1_conv_b, l1_dt_bias_f, l1_a_log_f, l1_dt_bias_b, l1_a_log_b, l1_d_skip, l1_ssm_norm, l1_gk_up_f, l1_gk_b_f, l1_gk_up_b, l1_gk_b_b, l1_gla_norm, final_norm):
    bsz, n_lat, d = x.shape
    n_ctx = ctx.shape[1]
    assert d == D and n_ctx % TM == 0 and n_lat % TM == 0 and n_lat % GRID_W == 0
    nct = n_ctx // TM

    mod0 = _modulation(c, c_ctx, l0_mod_w, l0_mod_b)
    qkv, pb = _inproj0((ctx, x), mod0, l0_norm1, l0_w_in, _rope_tables(n_ctx, n_lat), nct)
    lamv = jnp.stack([l0_lam_q1, l0_lam_k1, l0_lam_q2, l0_lam_k2], axis=0)
    oa = _attention(qkv, lamv, l0_subln, 0, n_ctx)
    yf_b, yb_b, bonus, gate = _rwkv_mix(pb, n_ctx, l0_mu, l0_w0_f, l0_w2_f, l0_w0_b, l0_w2_b, l0_a0, l0_a2, l0_g2,
                                        l0_k_k, l0_k_a, l0_r_k)
    xc = _post((ctx, x), _mix_even, [oa, yf_b, yb_b, bonus, gate],
               [l0_lnx_w.reshape(1, -1), l0_lnx_b.reshape(1, -1), _block_ones(B_WIDTH, B_HEAD)],
               mod0, l0_w_out, l0_norm2, l0_mlp_w1, l0_mlp_w2, None, nct, False)

    mod1 = _modulation(c, c_ctx, l1_mod_w, l1_mod_b)
    z, xact, qk, v, gg, small = _inproj1(xc, mod1, l1_norm1, l1_w_in, l1_conv_w, l1_conv_b, nct)
    yf_c, yb_c = _ssd_scan(xact, small, jnp.swapaxes(small, 1, 2), l1_dt_bias_f, l1_a_log_f,
                           l1_dt_bias_b, l1_a_log_b, n_ctx)
    yf_d, yb_d = _gla_scan(qk, v, small, l1_gk_up_f, l1_gk_b_f, l1_gk_up_b, l1_gk_b_b, n_ctx)
    return _post((xc,), _mix_odd, [yf_c, yb_c, xact, z, yf_d, yb_d, gg],
                 [jnp.repeat(l1_d_skip, C_HEAD).reshape(1, C_WIDTH), l1_ssm_norm.reshape(1, C_WIDTH),
                  l1_gla_norm.reshape(1, D_VAL)],
                 mod1, l1_w_out, l1_norm2, l1_mlp_w1, l1_mlp_w2, final_norm, nct, True)
```
